```python
import jax, jax.numpy as jnp
from jax import lax
import numpy as np

D_MODEL = 1024
BATCH = 8
SEQ = 2048
DEPTH = 1
DEC_BATCH = 128
DEC_SEQ = 8
PAST_LEN = 16384
PAGE_SIZE = 128

CONV_WIDTH = D_MODEL
CONV_W = 3
POOL_WIDTH = D_MODEL
POOL_WINDOWS = (2, 4, 8, 16)
N_POOL_GROUPS = len(POOL_WINDOWS)
POOL_GW = POOL_WIDTH // N_POOL_GROUPS
POOL_GW_OUT = D_MODEL // N_POOL_GROUPS
POOL_BUF = max(POOL_WINDOWS) - 1
D_FF = 4 * D_MODEL
PLE_DIM = 256
EPS = 1e-6
IN_COLS = 3 * CONV_WIDTH + POOL_WIDTH + 2 * D_MODEL

kernel_name = "hybrid_shortconv_pool_decoder_step"


def rms_norm(x, g):
    xf = x.astype(jnp.float32)
    xf = xf * lax.rsqrt(jnp.mean(xf * xf, axis=-1, keepdims=True) + EPS)
    return xf.astype(x.dtype) * g


def gated_short_conv(b, c, h, buf, w_conv):
    u = c * h
    ext = jnp.concatenate([buf, u], axis=1)
    T = u.shape[1]
    y = sum(ext[:, k:k + T] * w_conv[k] for k in range(CONV_W))
    return b * y, ext[:, -(CONV_W - 1):]


def multiscale_pool(v, buf, pos0, w_pool, pool_scale):
    Bn, T, _ = v.shape
    ext = jnp.concatenate([buf, v], axis=1)
    cs = jnp.cumsum(ext.astype(jnp.float32), axis=1)
    cs0 = jnp.concatenate([jnp.zeros((Bn, 1, POOL_WIDTH), jnp.float32), cs], axis=1)
    end = cs0[:, POOL_BUF + 1:POOL_BUF + 1 + T]
    pos = (pos0 + jnp.arange(T)).astype(jnp.float32)
    outs = []
    for gi, w in enumerate(POOL_WINDOWS):
        sl = slice(gi * POOL_GW, (gi + 1) * POOL_GW)
        s = end[..., sl] - cs0[:, POOL_BUF + 1 - w:POOL_BUF + 1 - w + T, sl]
        cnt = jnp.minimum(jnp.float32(w), pos + 1.0)[None, :, None]
        outs.append(s / cnt)
    pooled = jnp.concatenate(outs, axis=-1) - v.astype(jnp.float32)
    pooled = pooled.astype(v.dtype).reshape(Bn, T, N_POOL_GROUPS, POOL_GW)
    y = jnp.einsum('btgc,gcd->btgd', pooled, w_pool).reshape(Bn, T, D_MODEL)
    return y * pool_scale, ext[:, -POOL_BUF:]


def trunk(x, p, conv_bufs, pool_bufs, pos0, g_mix, w_in, w_conv, w_out_conv, w_pool,
          pool_scale, w_o, g_mlp, w_up, w_down, g_ple, w_ple_gate, w_ple_proj, g_final):
    h = x
    new_conv, new_pool = [], []
    o1, o2, o3 = CONV_WIDTH, 2 * CONV_WIDTH, 3 * CONV_WIDTH
    o4 = o3 + POOL_WIDTH
    o5 = o4 + D_MODEL
    for i in range(DEPTH):
        xn = rms_norm(h, g_mix[i])
        z = xn @ w_in[i]
        b, c, hc = z[..., :o1], z[..., o1:o2], z[..., o2:o3]
        v = z[..., o3:o4]
        gate_a = jax.nn.sigmoid(z[..., o4:o5])
        gate_b = jax.nn.sigmoid(z[..., o5:])
        ya, cb = gated_short_conv(b, c, hc, conv_bufs[i], w_conv[i])
        ya = ya @ w_out_conv[i]
        yb, pb = multiscale_pool(v, pool_bufs[i], pos0, w_pool[i], pool_scale[i])
        h = h + (gate_a * ya + gate_b * yb) @ w_o[i]
        new_conv.append(cb)
        new_pool.append(pb)
        hn = rms_norm(h, g_mlp[i])
        h = h + jnp.square(jax.nn.relu(hn @ w_up[i])) @ w_down[i]
        gp = jax.nn.sigmoid(rms_norm(h, g_ple[i]) @ w_ple_gate[i])
        h = h + gp * (p[i] @ w_ple_proj[i])
    return rms_norm(h, g_final), jnp.stack(new_conv), jnp.stack(new_pool)


def setup_inputs(seed: int = 0) -> dict:
    key = jax.random.key(seed)
    ks = jax.random.split(key, 24)
    f32 = jnp.float32

    def nrm(k, shape, scale):
        return jax.random.normal(k, shape, f32) * scale

    def gain(k, shape):
        return 1.0 + 0.05 * jax.random.normal(k, shape, f32)

    return {
        "x_prompt": nrm(ks[0], (BATCH, SEQ, D_MODEL), 1.0),
        "x_sample": nrm(ks[1], (DEC_BATCH, DEC_SEQ, D_MODEL), 1.0),
        "state_conv": nrm(ks[2], (DEPTH, DEC_BATCH, CONV_W - 1, CONV_WIDTH), 1.0),
        "state_pool": nrm(ks[3], (DEPTH, DEC_BATCH, POOL_BUF, POOL_WIDTH), 1.0),
        "p_prompt": nrm(ks[4], (DEPTH, BATCH, SEQ, PLE_DIM), 1.0),
        "p_sample": nrm(ks[5], (DEPTH, DEC_BATCH, DEC_SEQ, PLE_DIM), 1.0),
        "g_mix": gain(ks[6], (DEPTH, D_MODEL)),
        "w_in": nrm(ks[7], (DEPTH, D_MODEL, IN_COLS), D_MODEL ** -0.5),
        "w_conv": nrm(ks[8], (DEPTH, CONV_W, CONV_WIDTH), CONV_W ** -0.5),
        "w_out_conv": nrm(ks[9], (DEPTH, CONV_WIDTH, D_MODEL), CONV_WIDTH ** -0.5),
        "w_pool": nrm(ks[10], (DEPTH, N_POOL_GROUPS, POOL_GW, POOL_GW_OUT), POOL_GW ** -0.5),
        "pool_scale": gain(ks[11], (DEPTH, D_MODEL)),
        "w_o": nrm(ks[12], (DEPTH, D_MODEL, D_MODEL), D_MODEL ** -0.5),
        "g_mlp": gain(ks[13], (DEPTH, D_MODEL)),
        "w_up": nrm(ks[14], (DEPTH, D_MODEL, D_FF), D_MODEL ** -0.5),
        "w_down": nrm(ks[15], (DEPTH, D_FF, D_MODEL), D_FF ** -0.5),
        "g_ple": gain(ks[16], (DEPTH, D_MODEL)),
        "w_ple_gate": nrm(ks[17], (DEPTH, D_MODEL, D_MODEL), D_MODEL ** -0.5),
        "w_ple_proj": nrm(ks[18], (DEPTH, PLE_DIM, D_MODEL), PLE_DIM ** -0.5),
        "g_final": gain(ks[19], (D_MODEL,)),
    }


def reference(x_prompt, x_sample, state_conv, state_pool, p_prompt, p_sample, g_mix, w_in,
              w_conv, w_out_conv, w_pool, pool_scale, w_o, g_mlp, w_up, w_down, g_ple,
              w_ple_gate, w_ple_proj, g_final):
    zero_conv = jnp.zeros((DEPTH, x_prompt.shape[0], CONV_W - 1, CONV_WIDTH), x_prompt.dtype)
    zero_pool = jnp.zeros((DEPTH, x_prompt.shape[0], POOL_BUF, POOL_WIDTH), x_prompt.dtype)
    y_prompt, new_conv_prompt, new_pool_prompt = trunk(
        x_prompt, p_prompt, zero_conv, zero_pool, 0, g_mix, w_in, w_conv, w_out_conv,
        w_pool, pool_scale, w_o, g_mlp, w_up, w_down, g_ple, w_ple_gate, w_ple_proj, g_final)
    y_sample, new_conv_sample, new_pool_sample = trunk(
        x_sample, p_sample, state_conv, state_pool, PAST_LEN, g_mix, w_in, w_conv, w_out_conv,
        w_pool, pool_scale, w_o, g_mlp, w_up, w_down, g_ple, w_ple_gate, w_ple_proj, g_final)
    return (y_prompt, y_sample, new_conv_prompt, new_pool_prompt, new_conv_sample, new_pool_sample)
```

```python
import functools
from typing import NamedTuple

import jax
import jax.numpy as jnp
from jax import lax
from jax.experimental import pallas as pl
from jax.experimental.pallas import tpu as pltpu

CONV_W = 3
POOL_WINDOWS = (2, 4, 8, 16)
PAST_LEN = 16384
EPS = 1e-6

V7X_SUBLANES = 8
V7X_VMEM_BYTES = 64 * 1024 * 1024

_CONV_HIST = V7X_SUBLANES
_POOL_HIST = 2 * V7X_SUBLANES
_VMEM_LIMIT = V7X_VMEM_BYTES - 6 * 1024 * 1024


class _Cfg(NamedTuple):
    seqs: int
    rows: int
    carry: bool
    pos0: int
    d_ff_chunk: int


def _rms(x, g):
    ms = jnp.mean(x * x, axis=-1, keepdims=True)
    return x * lax.rsqrt(ms + EPS) * g


def _dot(a, w):
    return jnp.dot(a.astype(jnp.bfloat16), w, preferred_element_type=jnp.float32)


def _trunk_kernel(*refs, cfg: _Cfg):
    if cfg.carry:
        (x_ref, p_ref, g_mix, w_in, w_conv, w_oc, w_pool, pool_scale, w_o, g_mlp, w_up,
         w_down, g_ple, w_pg, w_pp, g_final, y_ref, nc_ref, np_ref, u_ext, v_ext) = refs
        sc_ref = sp_ref = None
    else:
        (x_ref, p_ref, sc_ref, sp_ref, g_mix, w_in, w_conv, w_oc, w_pool, pool_scale, w_o,
         g_mlp, w_up, w_down, g_ple, w_pg, w_pp, g_final, y_ref, nc_ref, np_ref, u_ext,
         v_ext) = refs

    S, L = cfg.seqs, cfg.rows
    R = S * L
    D = x_ref.shape[-1]
    C = w_conv.shape[-1]
    P = w_pool.shape[0] * w_pool.shape[1]
    GW = w_pool.shape[1]
    n_conv_hist = CONV_W - 1
    n_pool_hist = max(POOL_WINDOWS) - 1
    HU, HV = _CONV_HIST, _POOL_HIST

    if cfg.carry:
        j = pl.program_id(1)

        @pl.when(j == 0)
        def _():
            u_ext[:, 0:HU, :] = jnp.zeros((S, HU, C), jnp.float32)
            v_ext[:, 0:HV, :] = jnp.zeros((S, HV, P), jnp.float32)

        @pl.when(j > 0)
        def _():
            u_ext[:, 0:HU, :] = u_ext[:, L:L + HU, :]
            v_ext[:, 0:HV, :] = v_ext[:, L:L + HV, :]

        pos_base = cfg.pos0 + j * L
    else:
        u_ext[:, HU - n_conv_hist:HU, :] = sc_ref[...]
        v_ext[:, HV - n_pool_hist:HV, :] = sp_ref[...]
        pos_base = cfg.pos0

    x = x_ref[...].reshape(R, D)
    xn = _rms(x, g_mix[...]).astype(jnp.bfloat16)

    o1, o2, o3 = C, 2 * C, 3 * C
    o4 = o3 + P
    o5 = o4 + D

    b = _dot(xn, w_in[:, 0:o1])
    u = _dot(xn, w_in[:, o1:o2]) * _dot(xn, w_in[:, o2:o3])
    u_ext[:, HU:HU + L, :] = u.reshape(S, L, C)
    conv = None
    for k in range(CONV_W):
        lo = HU - n_conv_hist + k
        term = u_ext[:, lo:lo + L, :].reshape(R, C) * w_conv[k:k + 1, :]
        conv = term if conv is None else conv + term
    ya = _dot(b * conv, w_oc[...])

    v = _dot(xn, w_in[:, o3:o4])
    v_ext[:, HV:HV + L, :] = v.reshape(S, L, P)
    pos = (pos_base + lax.broadcasted_iota(jnp.int32, (S, L, GW), 1)).astype(jnp.float32)
    yb_groups = []
    for gi, w in enumerate(POOL_WINDOWS):
        sl = slice(gi * GW, (gi + 1) * GW)
        v_g = v_ext[:, HV:HV + L, sl]
        s = v_g
        for back in range(1, w):
            s = s + v_ext[:, HV - back:HV - back + L, sl]
        cnt = jnp.minimum(jnp.float32(w), pos + 1.0)
        pooled = (s / cnt - v_g).reshape(R, GW)
        yb_groups.append(_dot(pooled, w_pool[gi]))
    yb = jnp.concatenate(yb_groups, axis=-1) * pool_scale[...]

    gate_a = jax.nn.sigmoid(_dot(xn, w_in[:, o4:o5]))
    gate_b = jax.nn.sigmoid(_dot(xn, w_in[:, o5:]))
    h = x + _dot(gate_a * ya + gate_b * yb, w_o[...])

    def write_state():
        nc_ref[...] = u_ext[:, HU + L - n_conv_hist:HU + L, :]
        np_ref[...] = v_ext[:, HV + L - n_pool_hist:HV + L, :]

    if cfg.carry:
        pl.when(j == pl.num_programs(1) - 1)(write_state)
    else:
        write_state()

    hn = _rms(h, g_mlp[...]).astype(jnp.bfloat16)
    d_ff = w_up.shape[1]
    acc = h
    for c0 in range(0, d_ff, cfg.d_ff_chunk):
        mid = _dot(hn, w_up[:, c0:c0 + cfg.d_ff_chunk])
        acc = acc + _dot(jnp.square(jnp.maximum(mid, 0.0)), w_down[c0:c0 + cfg.d_ff_chunk, :])
    h = acc

    gp = jax.nn.sigmoid(_dot(_rms(h, g_ple[...]), w_pg[...]))
    pe = _dot(p_ref[...].reshape(R, p_ref.shape[-1]), w_pp[...])
    h = h + gp * pe

    y_ref[...] = _rms(h, g_final[...]).reshape(S, L, D)


def _resident(shape):
    zeros = (0,) * len(shape)
    return pl.BlockSpec(shape, lambda *_: zeros, pipeline_mode=pl.Buffered(1))


def _run_group(cfg, grid, tile_map, x, p, states, weights):
    S, L = cfg.seqs, cfg.rows
    n_seq, _, D = x.shape
    C = weights[2].shape[-1]
    P = weights[4].shape[0] * weights[4].shape[1]
    n_conv_hist = CONV_W - 1
    n_pool_hist = max(POOL_WINDOWS) - 1

    def seq_map(*idx):
        s, _, _ = tile_map(*idx)
        return (s, 0, 0)

    in_specs = [pl.BlockSpec((S, L, D), tile_map),
                pl.BlockSpec((S, L, p.shape[-1]), tile_map)]
    for st in states:
        in_specs.append(pl.BlockSpec((S,) + st.shape[1:], seq_map))
    in_specs += [_resident(w.shape) for w in weights]

    out_shape = (jax.ShapeDtypeStruct(x.shape, x.dtype),
                 jax.ShapeDtypeStruct((n_seq, n_conv_hist, C), x.dtype),
                 jax.ShapeDtypeStruct((n_seq, n_pool_hist, P), x.dtype))
    out_specs = (pl.BlockSpec((S, L, D), tile_map),
                 pl.BlockSpec((S, n_conv_hist, C), seq_map),
                 pl.BlockSpec((S, n_pool_hist, P), seq_map))
    scratch = [pltpu.VMEM((S, _CONV_HIST + L, C), jnp.float32),
               pltpu.VMEM((S, _POOL_HIST + L, P), jnp.float32)]

    return pl.pallas_call(
        functools.partial(_trunk_kernel, cfg=cfg),
        grid=grid,
        in_specs=in_specs,
        out_specs=out_specs,
        out_shape=out_shape,
        scratch_shapes=scratch,
        compiler_params=pltpu.CompilerParams(
            dimension_semantics=("arbitrary",) * len(grid),
            vmem_limit_bytes=_VMEM_LIMIT),
        name="trunk_carry" if cfg.carry else "trunk_state",
    )(x, p, *states, *weights)


def kernel(x_prompt, x_sample, state_conv, state_pool, p_prompt, p_sample, g_mix, w_in, w_conv, w_out_conv, w_pool, pool_scale, w_o, g_mlp, w_up, w_down, g_ple, w_ple_gate, w_ple_proj, g_final):
    depth = w_in.shape[0]
    assert depth == 1, "single-layer trunk only"
    bf16 = jnp.bfloat16
    weights = (g_mix, w_in[0].astype(bf16), w_conv[0], w_out_conv[0].astype(bf16),
               w_pool[0].astype(bf16), pool_scale, w_o[0].astype(bf16), g_mlp,
               w_up[0].astype(bf16), w_down[0].astype(bf16), g_ple,
               w_ple_gate[0].astype(bf16), w_ple_proj[0].astype(bf16), g_final[None, :])

    n_prompt, seq, _ = x_prompt.shape
    prompt_rows = 256
    assert seq % prompt_rows == 0 and prompt_rows >= _POOL_HIST
    cfg_p = _Cfg(seqs=1, rows=prompt_rows, carry=True, pos0=0, d_ff_chunk=1024)
    y_prompt, nc_prompt, np_prompt = _run_group(
        cfg_p, (n_prompt, seq // prompt_rows), lambda b, j: (b, j, 0),
        x_prompt, p_prompt[0], (), weights)

    n_sample, dec_seq, _ = x_sample.shape
    sample_seqs = 32
    assert n_sample % sample_seqs == 0 and dec_seq % V7X_SUBLANES == 0
    cfg_s = _Cfg(seqs=sample_seqs, rows=dec_seq, carry=False, pos0=PAST_LEN, d_ff_chunk=1024)
    y_sample, nc_sample, np_sample = _run_group(
        cfg_s, (n_sample // sample_seqs,), lambda i: (i, 0, 0),
        x_sample, p_sample[0], (state_conv[0], state_pool[0]), weights)

    return (y_prompt, y_sample, nc_prompt[None], np_prompt[None], nc_sample[None],
            np_sample[None])
```

```python
import functools
from typing import NamedTuple

import jax
import jax.numpy as jnp
from jax import lax
from jax.experimental import pallas as pl
from jax.experimental.pallas import tpu as pltpu

CONV_W = 3
POOL_WINDOWS = (2, 4, 8, 16)
PAST_LEN = 16384
EPS = 1e-6

V7X_SUBLANES = 8
V7X_VMEM_BYTES = 64 * 1024 * 1024

_CONV_HIST = V7X_SUBLANES
_POOL_HIST = 2 * V7X_SUBLANES
_VMEM_LIMIT = V7X_VMEM_BYTES - 6 * 1024 * 1024


class _Cfg(NamedTuple):
    seqs: int
    rows: int
    carry: bool
    pos0: int
    d_ff_chunk: int


def _rms(x, g):
    ms = jnp.mean(x * x, axis=-1, keepdims=True)
    return x * lax.rsqrt(ms + EPS) * g


def _dot(a, w):
    return jnp.dot(a.astype(jnp.bfloat16), w, preferred_element_type=jnp.float32)


def _trunk_kernel(*refs, cfg: _Cfg):
    if cfg.carry:
        (x_ref, p_ref, g_mix, w_in, w_conv, w_oc, w_pool, pool_scale, w_o, g_mlp, w_up,
         w_down, g_ple, w_pg, w_pp, g_final, y_ref, nc_ref, np_ref, u_ext, v_ext) = refs
        sc_ref = sp_ref = None
    else:
        (x_ref, p_ref, sc_ref, sp_ref, g_mix, w_in, w_conv, w_oc, w_pool, pool_scale, w_o,
         g_mlp, w_up, w_down, g_ple, w_pg, w_pp, g_final, y_ref, nc_ref, np_ref, u_ext,
         v_ext) = refs

    S, L = cfg.seqs, cfg.rows
    R = S * L
    D = x_ref.shape[-1]
    C = w_conv.shape[-1]
    P = w_pool.shape[0] * w_pool.shape[1]
    GW = w_pool.shape[1]
    n_conv_hist = CONV_W - 1
    n_pool_hist = max(POOL_WINDOWS) - 1
    HU, HV = _CONV_HIST, _POOL_HIST

    if cfg.carry:
        j = pl.program_id(1)

        @pl.when(j == 0)
        def _():
            u_ext[:, 0:HU, :] = jnp.zeros((S, HU, C), jnp.float32)
            v_ext[:, 0:HV, :] = jnp.zeros((S, HV, P), jnp.float32)

        @pl.when(j > 0)
        def _():
            u_ext[:, 0:HU, :] = u_ext[:, L:L + HU, :]
            v_ext[:, 0:HV, :] = v_ext[:, L:L + HV, :]

        pos_base = cfg.pos0 + j * L
    else:
        u_ext[:, HU - n_conv_hist:HU, :] = sc_ref[...]
        v_ext[:, HV - n_pool_hist:HV, :] = sp_ref[...]
        pos_base = cfg.pos0

    x = x_ref[...].reshape(R, D)
    xn = _rms(x, g_mix[...]).astype(jnp.bfloat16)

    o1, o2, o3 = C, 2 * C, 3 * C
    o4 = o3 + P
    o5 = o4 + D

    b = _dot(xn, w_in[:, 0:o1])
    u = _dot(xn, w_in[:, o1:o2]) * _dot(xn, w_in[:, o2:o3])
    u_ext[:, HU:HU + L, :] = u.reshape(S, L, C)
    conv = None
    for k in range(CONV_W):
        lo = HU - n_conv_hist + k
        term = u_ext[:, lo:lo + L, :].reshape(R, C) * w_conv[k:k + 1, :]
        conv = term if conv is None else conv + term
    ya = _dot(b * conv, w_oc[...])

    v = _dot(xn, w_in[:, o3:o4])
    v_ext[:, HV:HV + L, :] = v.reshape(S, L, P)
    pos = (pos_base + lax.broadcasted_iota(jnp.int32, (S, L, GW), 1)).astype(jnp.float32)
    yb_groups = []
    for gi, w in enumerate(POOL_WINDOWS):
        sl = slice(gi * GW, (gi + 1) * GW)
        v_g = v_ext[:, HV:HV + L, sl]
        s = v_g
        for back in range(1, w):
            s = s + v_ext[:, HV - back:HV - back + L, sl]
        cnt = jnp.minimum(jnp.float32(w), pos + 1.0)
        pooled = (s / cnt - v_g).reshape(R, GW)
        yb_groups.append(_dot(pooled, w_pool[gi]))
    yb = jnp.concatenate(yb_groups, axis=-1) * pool_scale[...]

    gate_a = jax.nn.sigmoid(_dot(xn, w_in[:, o4:o5]))
    gate_b = jax.nn.sigmoid(_dot(xn, w_in[:, o5:]))
    h = x + _dot(gate_a * ya + gate_b * yb, w_o[...])

    def write_state():
        nc_ref[...] = u_ext[:, HU + L - n_conv_hist:HU + L, :]
        np_ref[...] = v_ext[:, HV + L - n_pool_hist:HV + L, :]

    if cfg.carry:
        pl.when(j == pl.num_programs(1) - 1)(write_state)
    else:
        write_state()

    hn = _rms(h, g_mlp[...]).astype(jnp.bfloat16)
    d_ff = w_up.shape[1]
    acc = h
    for c0 in range(0, d_ff, cfg.d_ff_chunk):
        mid = _dot(hn, w_up[:, c0:c0 + cfg.d_ff_chunk])
        acc = acc + _dot(jnp.square(jnp.maximum(mid, 0.0)), w_down[c0:c0 + cfg.d_ff_chunk, :])
    h = acc

    gp = jax.nn.sigmoid(_dot(_rms(h, g_ple[...]), w_pg[...]))
    pe = _dot(p_ref[...].reshape(R, p_ref.shape[-1]), w_pp[...])
    h = h + gp * pe

    y_ref[...] = _rms(h, g_final[...]).reshape(S, L, D)


def _resident(shape):
    zeros = (0,) * len(shape)
    return pl.BlockSpec(shape, lambda *_: zeros, pipeline_mode=pl.Buffered(1))


def _run_group(cfg, grid, tile_map, x, p, states, weights):
    S, L = cfg.seqs, cfg.rows
    n_seq, _, D = x.shape
    C = weights[2].shape[-1]
    P = weights[4].shape[0] * weights[4].shape[1]
    n_conv_hist = CONV_W - 1
    n_pool_hist = max(POOL_WINDOWS) - 1

    def seq_map(*idx):
        s, _, _ = tile_map(*idx)
        return (s, 0, 0)

    in_specs = [pl.BlockSpec((S, L, D), tile_map),
                pl.BlockSpec((S, L, p.shape[-1]), tile_map)]
    for st in states:
        in_specs.append(pl.BlockSpec((S,) + st.shape[1:], seq_map))
    in_specs += [_resident(w.shape) for w in weights]

    out_shape = (jax.ShapeDtypeStruct(x.shape, x.dtype),
                 jax.ShapeDtypeStruct((n_seq, n_conv_hist, C), x.dtype),
                 jax.ShapeDtypeStruct((n_seq, n_pool_hist, P), x.dtype))
    out_specs = (pl.BlockSpec((S, L, D), tile_map),
                 pl.BlockSpec((S, n_conv_hist, C), seq_map),
                 pl.BlockSpec((S, n_pool_hist, P), seq_map))
    scratch = [pltpu.VMEM((S, _CONV_HIST + L, C), jnp.float32),
               pltpu.VMEM((S, _POOL_HIST + L, P), jnp.float32)]

    return pl.pallas_call(
        functools.partial(_trunk_kernel, cfg=cfg),
        grid=grid,
        in_specs=in_specs,
        out_specs=out_specs,
        out_shape=out_shape,
        scratch_shapes=scratch,
        compiler_params=pltpu.CompilerParams(
            dimension_semantics=("arbitrary",) * len(grid),
            vmem_limit_bytes=_VMEM_LIMIT),
        name="trunk_carry" if cfg.carry else "trunk_state",
    )(x, p, *states, *weights)


def kernel(x_prompt, x_sample, state_conv, state_pool, p_prompt, p_sample, g_mix, w_in, w_conv, w_out_conv, w_pool, pool_scale, w_o, g_mlp, w_up, w_down, g_ple, w_ple_gate, w_ple_proj, g_final):
    depth = w_in.shape[0]
    assert depth == 1, "single-layer trunk only"
    bf16 = jnp.bfloat16
    weights = (g_mix, w_in[0].astype(bf16), w_conv[0], w_out_conv[0].astype(bf16),
               w_pool[0].astype(bf16), pool_scale, w_o[0].astype(bf16), g_mlp,
               w_up[0].astype(bf16), w_down[0].astype(bf16), g_ple,
               w_ple_gate[0].astype(bf16), w_ple_proj[0].astype(bf16), g_final[None, :])

    n_prompt, seq, _ = x_prompt.shape
    prompt_rows = 512
    assert seq % prompt_rows == 0 and prompt_rows >= _POOL_HIST
    cfg_p = _Cfg(seqs=1, rows=prompt_rows, carry=True, pos0=0, d_ff_chunk=1024)
    y_prompt, nc_prompt, np_prompt = _run_group(
        cfg_p, (n_prompt, seq // prompt_rows), lambda b, j: (b, j, 0),
        x_prompt, p_prompt[0], (), weights)

    n_sample, dec_seq, _ = x_sample.shape
    sample_seqs = 32
    assert n_sample % sample_seqs == 0 and dec_seq % V7X_SUBLANES == 0
    cfg_s = _Cfg(seqs=sample_seqs, rows=dec_seq, carry=False, pos0=PAST_LEN, d_ff_chunk=1024)
    y_sample, nc_sample, np_sample = _run_group(
        cfg_s, (n_sample // sample_seqs,), lambda i: (i, 0, 0),
        x_sample, p_sample[0], (state_conv[0], state_pool[0]), weights)

    return (y_prompt, y_sample, nc_prompt[None], np_prompt[None], nc_sample[None],
            np_sample[None])
```

```python
import functools
from typing import NamedTuple

import jax
import jax.numpy as jnp
from jax import lax
from jax.experimental import pallas as pl
from jax.experimental.pallas import tpu as pltpu

CONV_W = 3
POOL_WINDOWS = (2, 4, 8, 16)
PAST_LEN = 16384
EPS = 1e-6

V7X_SUBLANES = 8
V7X_LANES = 128
V7X_VMEM_BYTES = 64 * 1024 * 1024

_CONV_HIST = V7X_SUBLANES
_POOL_HIST = 2 * V7X_SUBLANES
_VMEM_LIMIT = V7X_VMEM_BYTES - 6 * 1024 * 1024


class _Cfg(NamedTuple):
    seqs: int
    rows: int
    carry: bool
    pos0: int
    d_ff_chunk: int


def _rms(x, g):
    ms = jnp.mean(x * x, axis=-1, keepdims=True)
    return x * lax.rsqrt(ms + EPS) * g


def _sigmoid(x):
    return 0.5 * jnp.tanh(0.5 * x) + 0.5


def _dot(a, w):
    return jnp.dot(a.astype(jnp.bfloat16), w, preferred_element_type=jnp.float32)


def _trunk_kernel(*refs, cfg: _Cfg):
    if cfg.carry:
        (x_ref, p_ref, g_mix, w_in, w_conv, w_oc, w_pool, pool_scale, w_o, g_mlp, w_up,
         w_down, g_ple, w_pg, w_pp, g_final, y_ref, nc_ref, np_ref, u_ext, v_ext) = refs
        sc_ref = sp_ref = None
    else:
        (x_ref, p_ref, sc_ref, sp_ref, g_mix, w_in, w_conv, w_oc, w_pool, pool_scale, w_o,
         g_mlp, w_up, w_down, g_ple, w_pg, w_pp, g_final, y_ref, nc_ref, np_ref, u_ext,
         v_ext) = refs

    S, L = cfg.seqs, cfg.rows
    R = S * L
    D = x_ref.shape[-1]
    C = w_conv.shape[-1]
    P = w_pool.shape[0] * w_pool.shape[1]
    GW = w_pool.shape[1]
    n_conv_hist = CONV_W - 1
    n_pool_hist = max(POOL_WINDOWS) - 1
    HU, HV = _CONV_HIST, _POOL_HIST

    if cfg.carry:
        j = pl.program_id(1)

        @pl.when(j == 0)
        def _():
            u_ext[:, 0:HU, :] = jnp.zeros((S, HU, C), jnp.float32)
            v_ext[:, 0:HV, :] = jnp.zeros((S, HV, P), jnp.float32)

        @pl.when(j > 0)
        def _():
            u_ext[:, 0:HU, :] = u_ext[:, L:L + HU, :]
            v_ext[:, 0:HV, :] = v_ext[:, L:L + HV, :]

        pos_base = cfg.pos0 + j * L
    else:
        u_ext[:, HU - n_conv_hist:HU, :] = sc_ref[...]
        v_ext[:, 0:HV - n_pool_hist, :] = jnp.zeros((S, HV - n_pool_hist, P), jnp.float32)
        v_ext[:, HV - n_pool_hist:HV, :] = sp_ref[...]
        pos_base = cfg.pos0

    x = x_ref[...].reshape(R, D)
    xn = _rms(x, g_mix[...]).astype(jnp.bfloat16)

    o1, o2, o3 = C, 2 * C, 3 * C
    o4 = o3 + P
    o5 = o4 + D

    b = _dot(xn, w_in[:, 0:o1])
    u = _dot(xn, w_in[:, o1:o2]) * _dot(xn, w_in[:, o2:o3])
    u_ext[:, HU:HU + L, :] = u.reshape(S, L, C)
    conv = None
    for k in range(CONV_W):
        lo = HU - n_conv_hist + k
        term = u_ext[:, lo:lo + L, :].reshape(R, C) * w_conv[k:k + 1, :]
        conv = term if conv is None else conv + term
    ya = _dot(b * conv, w_oc[...])

    v = _dot(xn, w_in[:, o3:o4])
    v_ext[:, HV:HV + L, :] = v.reshape(S, L, P)
    pos = (pos_base + lax.broadcasted_iota(jnp.int32, (S, L, V7X_LANES), 1)).astype(jnp.float32)
    yb_groups = []
    for gi, w in enumerate(POOL_WINDOWS):
        sl = slice(gi * GW, (gi + 1) * GW)
        s = v_ext[:, :, sl]
        k = 1
        while k < w:
            s = s + pltpu.roll(s, k, axis=1)
            k *= 2
        inv_cnt = 1.0 / jnp.minimum(jnp.float32(w), pos + 1.0)
        inv_cnt = jnp.concatenate([inv_cnt] * (GW // V7X_LANES), axis=-1)
        pooled = (s[:, HV:HV + L, :] * inv_cnt - v_ext[:, HV:HV + L, sl]).reshape(R, GW)
        yb_groups.append(_dot(pooled, w_pool[gi]))
    yb = jnp.concatenate(yb_groups, axis=-1) * pool_scale[...]

    gate_a = _sigmoid(_dot(xn, w_in[:, o4:o5]))
    gate_b = _sigmoid(_dot(xn, w_in[:, o5:]))
    h = x + _dot(gate_a * ya + gate_b * yb, w_o[...])

    def write_state():
        nc_ref[...] = u_ext[:, HU + L - n_conv_hist:HU + L, :]
        np_ref[...] = v_ext[:, HV + L - n_pool_hist:HV + L, :]

    if cfg.carry:
        pl.when(j == pl.num_programs(1) - 1)(write_state)
    else:
        write_state()

    hn = _rms(h, g_mlp[...]).astype(jnp.bfloat16)
    d_ff = w_up.shape[1]
    acc = h
    for c0 in range(0, d_ff, cfg.d_ff_chunk):
        mid = _dot(hn, w_up[:, c0:c0 + cfg.d_ff_chunk])
        acc = acc + _dot(jnp.square(jnp.maximum(mid, 0.0)), w_down[c0:c0 + cfg.d_ff_chunk, :])
    h = acc

    gp = _sigmoid(_dot(_rms(h, g_ple[...]), w_pg[...]))
    pe = _dot(p_ref[...].reshape(R, p_ref.shape[-1]), w_pp[...])
    h = h + gp * pe

    y_ref[...] = _rms(h, g_final[...]).reshape(S, L, D)


def _resident(shape):
    zeros = (0,) * len(shape)
    return pl.BlockSpec(shape, lambda *_: zeros, pipeline_mode=pl.Buffered(1))


def _run_group(cfg, grid, tile_map, x, p, states, weights):
    S, L = cfg.seqs, cfg.rows
    n_seq, _, D = x.shape
    C = weights[2].shape[-1]
    P = weights[4].shape[0] * weights[4].shape[1]
    n_conv_hist = CONV_W - 1
    n_pool_hist = max(POOL_WINDOWS) - 1

    def seq_map(*idx):
        s, _, _ = tile_map(*idx)
        return (s, 0, 0)

    in_specs = [pl.BlockSpec((S, L, D), tile_map),
                pl.BlockSpec((S, L, p.shape[-1]), tile_map)]
    for st in states:
        in_specs.append(pl.BlockSpec((S,) + st.shape[1:], seq_map))
    in_specs += [_resident(w.shape) for w in weights]

    out_shape = (jax.ShapeDtypeStruct(x.shape, x.dtype),
                 jax.ShapeDtypeStruct((n_seq, n_conv_hist, C), x.dtype),
                 jax.ShapeDtypeStruct((n_seq, n_pool_hist, P), x.dtype))
    out_specs = (pl.BlockSpec((S, L, D), tile_map),
                 pl.BlockSpec((S, n_conv_hist, C), seq_map),
                 pl.BlockSpec((S, n_pool_hist, P), seq_map))
    scratch = [pltpu.VMEM((S, _CONV_HIST + L, C), jnp.float32),
               pltpu.VMEM((S, _POOL_HIST + L, P), jnp.float32)]

    return pl.pallas_call(
        functools.partial(_trunk_kernel, cfg=cfg),
        grid=grid,
        in_specs=in_specs,
        out_specs=out_specs,
        out_shape=out_shape,
        scratch_shapes=scratch,
        compiler_params=pltpu.CompilerParams(
            dimension_semantics=("arbitrary",) * len(grid),
            vmem_limit_bytes=_VMEM_LIMIT),
        name="trunk_carry" if cfg.carry else "trunk_state",
    )(x, p, *states, *weights)


def kernel(x_prompt, x_sample, state_conv, state_pool, p_prompt, p_sample, g_mix, w_in, w_conv, w_out_conv, w_pool, pool_scale, w_o, g_mlp, w_up, w_down, g_ple, w_ple_gate, w_ple_proj, g_final):
    depth = w_in.shape[0]
    assert depth == 1, "single-layer trunk only"
    bf16 = jnp.bfloat16
    weights = (g_mix, w_in[0].astype(bf16), w_conv[0], w_out_conv[0].astype(bf16),
               w_pool[0].astype(bf16), pool_scale, w_o[0].astype(bf16), g_mlp,
               w_up[0].astype(bf16), w_down[0].astype(bf16), g_ple,
               w_ple_gate[0].astype(bf16), w_ple_proj[0].astype(bf16), g_final[None, :])

    n_prompt, seq, _ = x_prompt.shape
    prompt_rows = 512
    assert seq % prompt_rows == 0 and prompt_rows >= _POOL_HIST
    cfg_p = _Cfg(seqs=1, rows=prompt_rows, carry=True, pos0=0, d_ff_chunk=1024)
    y_prompt, nc_prompt, np_prompt = _run_group(
        cfg_p, (n_prompt, seq // prompt_rows), lambda b, j: (b, j, 0),
        x_prompt, p_prompt[0], (), weights)

    n_sample, dec_seq, _ = x_sample.shape
    sample_seqs = 32
    assert n_sample % sample_seqs == 0 and dec_seq % V7X_SUBLANES == 0
    cfg_s = _Cfg(seqs=sample_seqs, rows=dec_seq, carry=False, pos0=PAST_LEN, d_ff_chunk=1024)
    y_sample, nc_sample, np_sample = _run_group(
        cfg_s, (n_sample // sample_seqs,), lambda i: (i, 0, 0),
        x_sample, p_sample[0], (state_conv[0], state_pool[0]), weights)

    return (y_prompt, y_sample, nc_prompt[None], np_prompt[None], nc_sample[None],
            np_sample[None])
```

```python
import functools
from typing import NamedTuple

import jax
import jax.numpy as jnp
from jax import lax
from jax.experimental import pallas as pl
from jax.experimental.pallas import tpu as pltpu

CONV_W = 3
POOL_WINDOWS = (2, 4, 8, 16)
PAST_LEN = 16384
EPS = 1e-6

V7X_SUBLANES = 8
V7X_LANES = 128
V7X_VMEM_BYTES = 64 * 1024 * 1024

_CONV_HIST = V7X_SUBLANES
_POOL_HIST = 2 * V7X_SUBLANES
_VMEM_LIMIT = V7X_VMEM_BYTES - 6 * 1024 * 1024

_STAGE_ROWS, _STAGE_COLS, _STAGE_SLOTS = 256, 1024, 3

_SMALL = ("g_mix", "w_conv", "pool_scale", "g_mlp", "g_ple", "g_final")
_BIG = ("w_in", "w_oc", "w_pool", "w_o", "w_up", "w_down", "w_pg", "w_pp")


class _Cfg(NamedTuple):
    seqs: int
    rows: int
    carry: bool
    stage: bool
    pos0: int
    d_ff_chunk: int


def _rms(x, g):
    ms = jnp.mean(x * x, axis=-1, keepdims=True)
    return x * lax.rsqrt(ms + EPS) * g


def _sigmoid(x):
    return 0.5 * jnp.tanh(0.5 * x) + 0.5


def _dot(a, w):
    return jnp.dot(a.astype(jnp.bfloat16), w, preferred_element_type=jnp.float32)


def _stage_chunks(shape):
    lead = range(shape[0]) if len(shape) == 3 else (None,)
    rows, cols = shape[-2:]
    for g in lead:
        for r0 in range(0, rows, _STAGE_ROWS):
            for c0 in range(0, cols, _STAGE_COLS):
                yield (g, slice(r0, min(r0 + _STAGE_ROWS, rows)),
                       slice(c0, min(c0 + _STAGE_COLS, cols)))


def _round_weights_to_vmem(hbm, vmem, stage, sem):
    jobs = []
    for name in _BIG:
        for g, rs, cs in _stage_chunks(hbm[name].shape):
            src = hbm[name] if g is None else hbm[name].at[g]
            dst = vmem[name] if g is None else vmem[name].at[g]
            jobs.append((src.at[rs, cs], dst, rs, cs))

    def copy(n):
        src, _, rs, cs = jobs[n]
        slot = n % _STAGE_SLOTS
        return pltpu.make_async_copy(
            src, stage.at[slot, 0:rs.stop - rs.start, 0:cs.stop - cs.start], sem.at[slot])

    for n in range(min(_STAGE_SLOTS - 1, len(jobs))):
        copy(n).start()
    for n, (_, dst, rs, cs) in enumerate(jobs):
        if n + _STAGE_SLOTS - 1 < len(jobs):
            copy(n + _STAGE_SLOTS - 1).start()
        copy(n).wait()
        slot = n % _STAGE_SLOTS
        dst[rs, cs] = stage[slot, 0:rs.stop - rs.start, 0:cs.stop - cs.start].astype(jnp.bfloat16)


def _trunk_kernel(*refs, cfg: _Cfg):
    names = ["x", "p"] + ([] if cfg.carry else ["sc", "sp"]) + list(_SMALL) + list(_BIG)
    names += ["y", "nc", "np"] + (["out_" + n for n in _BIG] if cfg.stage else [])
    names += ["u_ext", "v_ext"]
    if cfg.stage:
        names += ["vm_" + n for n in _BIG] + ["stage", "stage_sem", "out_sem"]
    assert len(names) == len(refs)
    r = dict(zip(names, refs))
    x_ref, p_ref, y_ref, nc_ref, np_ref = r["x"], r["p"], r["y"], r["nc"], r["np"]
    u_ext, v_ext = r["u_ext"], r["v_ext"]
    g_mix, w_conv, pool_scale, g_mlp, g_ple, g_final = (r[n] for n in _SMALL)

    if cfg.stage:
        hbm = {n: r[n] for n in _BIG}
        vmem = {n: r["vm_" + n] for n in _BIG}
        step = pl.program_id(0) * pl.num_programs(1) + pl.program_id(1)
        n_steps = pl.num_programs(0) * pl.num_programs(1)

        def export(i, n):
            return pltpu.make_async_copy(vmem[n], r["out_" + n], r["out_sem"].at[i])

        @pl.when(step == 0)
        def _():
            _round_weights_to_vmem(hbm, vmem, r["stage"], r["stage_sem"])
            for i, n in enumerate(_BIG):
                export(i, n).start()

        w_in, w_oc, w_pool, w_o, w_up, w_down, w_pg, w_pp = (vmem[n] for n in _BIG)
    else:
        w_in, w_oc, w_pool, w_o, w_up, w_down, w_pg, w_pp = (r[n] for n in _BIG)

    S, L = cfg.seqs, cfg.rows
    R = S * L
    D = x_ref.shape[-1]
    C = w_conv.shape[-1]
    P = w_pool.shape[0] * w_pool.shape[1]
    GW = w_pool.shape[1]
    n_conv_hist = CONV_W - 1
    n_pool_hist = max(POOL_WINDOWS) - 1
    HU, HV = _CONV_HIST, _POOL_HIST

    if cfg.carry:
        j = pl.program_id(1)

        @pl.when(j == 0)
        def _():
            u_ext[:, 0:HU, :] = jnp.zeros((S, HU, C), jnp.float32)
            v_ext[:, 0:HV, :] = jnp.zeros((S, HV, P), jnp.float32)

        @pl.when(j > 0)
        def _():
            u_ext[:, 0:HU, :] = u_ext[:, L:L + HU, :]
            v_ext[:, 0:HV, :] = v_ext[:, L:L + HV, :]

        pos_base = cfg.pos0 + j * L
    else:
        u_ext[:, HU - n_conv_hist:HU, :] = r["sc"][...]
        v_ext[:, 0:HV - n_pool_hist, :] = jnp.zeros((S, HV - n_pool_hist, P), jnp.float32)
        v_ext[:, HV - n_pool_hist:HV, :] = r["sp"][...]
        pos_base = cfg.pos0

    x = x_ref[...].reshape(R, D)
    xn = _rms(x, g_mix[...]).astype(jnp.bfloat16)

    o1, o2, o3 = C, 2 * C, 3 * C
    o4 = o3 + P
    o5 = o4 + D

    b = _dot(xn, w_in[:, 0:o1])
    u = _dot(xn, w_in[:, o1:o2]) * _dot(xn, w_in[:, o2:o3])
    u_ext[:, HU:HU + L, :] = u.reshape(S, L, C)
    conv = None
    for k in range(CONV_W):
        lo = HU - n_conv_hist + k
        term = u_ext[:, lo:lo + L, :].reshape(R, C) * w_conv[k:k + 1, :]
        conv = term if conv is None else conv + term
    ya = _dot(b * conv, w_oc[...])

    v = _dot(xn, w_in[:, o3:o4])
    v_ext[:, HV:HV + L, :] = v.reshape(S, L, P)
    pos = (pos_base + lax.broadcasted_iota(jnp.int32, (S, L, V7X_LANES), 1)).astype(jnp.float32)
    yb_groups = []
    for gi, w in enumerate(POOL_WINDOWS):
        sl = slice(gi * GW, (gi + 1) * GW)
        s = v_ext[:, :, sl]
        k = 1
        while k < w:
            s = s + pltpu.roll(s, k, axis=1)
            k *= 2
        inv_cnt = 1.0 / jnp.minimum(jnp.float32(w), pos + 1.0)
        inv_cnt = jnp.concatenate([inv_cnt] * (GW // V7X_LANES), axis=-1)
        pooled = (s[:, HV:HV + L, :] * inv_cnt - v_ext[:, HV:HV + L, sl]).reshape(R, GW)
        yb_groups.append(_dot(pooled, w_pool[gi]))
    yb = jnp.concatenate(yb_groups, axis=-1) * pool_scale[...]

    gate_a = _sigmoid(_dot(xn, w_in[:, o4:o5]))
    gate_b = _sigmoid(_dot(xn, w_in[:, o5:]))
    h = x + _dot(gate_a * ya + gate_b * yb, w_o[...])

    def write_state():
        nc_ref[...] = u_ext[:, HU + L - n_conv_hist:HU + L, :]
        np_ref[...] = v_ext[:, HV + L - n_pool_hist:HV + L, :]

    if cfg.carry:
        pl.when(j == pl.num_programs(1) - 1)(write_state)
    else:
        write_state()

    hn = _rms(h, g_mlp[...]).astype(jnp.bfloat16)
    d_ff = w_up.shape[1]
    acc = h
    for c0 in range(0, d_ff, cfg.d_ff_chunk):
        mid = _dot(hn, w_up[:, c0:c0 + cfg.d_ff_chunk])
        acc = acc + _dot(jnp.square(jnp.maximum(mid, 0.0)), w_down[c0:c0 + cfg.d_ff_chunk, :])
    h = acc

    gp = _sigmoid(_dot(_rms(h, g_ple[...]), w_pg[...]))
    pe = _dot(p_ref[...].reshape(R, p_ref.shape[-1]), w_pp[...])
    h = h + gp * pe

    y_ref[...] = _rms(h, g_final[...]).reshape(S, L, D)

    if cfg.stage:
        @pl.when(step == n_steps - 1)
        def _():
            for i, n in enumerate(_BIG):
                export(i, n).wait()


def _resident(shape):
    zeros = (0,) * len(shape)
    return pl.BlockSpec(shape, lambda *_: zeros, pipeline_mode=pl.Buffered(1))


def _run_group(cfg, grid, tile_map, x, p, states, small, big):
    S, L = cfg.seqs, cfg.rows
    n_seq, _, D = x.shape
    C = small[1].shape[-1]
    P = big[2].shape[0] * big[2].shape[1]
    n_conv_hist = CONV_W - 1
    n_pool_hist = max(POOL_WINDOWS) - 1
    any_spec = pl.BlockSpec(memory_space=pl.ANY)

    def seq_map(*idx):
        s, _, _ = tile_map(*idx)
        return (s, 0, 0)

    in_specs = [pl.BlockSpec((S, L, D), tile_map),
                pl.BlockSpec((S, L, p.shape[-1]), tile_map)]
    for st in states:
        in_specs.append(pl.BlockSpec((S,) + st.shape[1:], seq_map))
    in_specs += [_resident(w.shape) for w in small]
    in_specs += [any_spec if cfg.stage else _resident(w.shape) for w in big]

    out_shape = [jax.ShapeDtypeStruct(x.shape, x.dtype),
                 jax.ShapeDtypeStruct((n_seq, n_conv_hist, C), x.dtype),
                 jax.ShapeDtypeStruct((n_seq, n_pool_hist, P), x.dtype)]
    out_specs = [pl.BlockSpec((S, L, D), tile_map),
                 pl.BlockSpec((S, n_conv_hist, C), seq_map),
                 pl.BlockSpec((S, n_pool_hist, P), seq_map)]
    scratch = [pltpu.VMEM((S, _CONV_HIST + L, C), jnp.float32),
               pltpu.VMEM((S, _POOL_HIST + L, P), jnp.float32)]
    if cfg.stage:
        out_shape += [jax.ShapeDtypeStruct(w.shape, jnp.bfloat16) for w in big]
        out_specs += [any_spec for _ in big]
        scratch += [pltpu.VMEM(w.shape, jnp.bfloat16) for w in big]
        scratch += [pltpu.VMEM((_STAGE_SLOTS, _STAGE_ROWS, _STAGE_COLS), jnp.float32),
                    pltpu.SemaphoreType.DMA((_STAGE_SLOTS,)),
                    pltpu.SemaphoreType.DMA((len(big),))]

    outs = pl.pallas_call(
        functools.partial(_trunk_kernel, cfg=cfg),
        grid=grid,
        in_specs=in_specs,
        out_specs=out_specs,
        out_shape=out_shape,
        scratch_shapes=scratch,
        compiler_params=pltpu.CompilerParams(
            dimension_semantics=("arbitrary",) * len(grid),
            vmem_limit_bytes=_VMEM_LIMIT),
        name="trunk_carry" if cfg.carry else "trunk_state",
    )(x, p, *states, *small, *big)
    return outs[:3], tuple(outs[3:])


def kernel(x_prompt, x_sample, state_conv, state_pool, p_prompt, p_sample, g_mix, w_in, w_conv, w_out_conv, w_pool, pool_scale, w_o, g_mlp, w_up, w_down, g_ple, w_ple_gate, w_ple_proj, g_final):
    depth = w_in.shape[0]
    assert depth == 1, "single-layer trunk only"
    small = (g_mix, w_conv[0], pool_scale, g_mlp, g_ple, g_final[None, :])
    big_f32 = (w_in[0], w_out_conv[0], w_pool[0], w_o[0], w_up[0], w_down[0], w_ple_gate[0],
               w_ple_proj[0])

    n_prompt, seq, _ = x_prompt.shape
    prompt_rows = 512
    assert seq % prompt_rows == 0 and prompt_rows >= _POOL_HIST
    cfg_p = _Cfg(seqs=1, rows=prompt_rows, carry=True, stage=True, pos0=0, d_ff_chunk=1024)
    (y_prompt, nc_prompt, np_prompt), big_bf16 = _run_group(
        cfg_p, (n_prompt, seq // prompt_rows), lambda b, j: (b, j, 0),
        x_prompt, p_prompt[0], (), small, big_f32)

    n_sample, dec_seq, _ = x_sample.shape
    sample_seqs = 32
    assert n_sample % sample_seqs == 0 and dec_seq % V7X_SUBLANES == 0
    cfg_s = _Cfg(seqs=sample_seqs, rows=dec_seq, carry=False, stage=False, pos0=PAST_LEN,
                 d_ff_chunk=1024)
    (y_sample, nc_sample, np_sample), _ = _run_group(
        cfg_s, (n_sample // sample_seqs,), lambda i: (i, 0, 0),
        x_sample, p_sample[0], (state_conv[0], state_pool[0]), small, big_bf16)

    return (y_prompt, y_sample, nc_prompt[None], np_prompt[None], nc_sample[None],
            np_sample[None])
```

```python
import functools
from typing import NamedTuple

import jax
import jax.numpy as jnp
from jax import lax
from jax.experimental import pallas as pl
from jax.experimental.pallas import tpu as pltpu

CONV_W = 3
POOL_WINDOWS = (2, 4, 8, 16)
PAST_LEN = 16384
EPS = 1e-6

V7X_SUBLANES = 8
V7X_LANES = 128
V7X_VMEM_BYTES = 64 * 1024 * 1024

_CONV_HIST = V7X_SUBLANES
_POOL_HIST = 2 * V7X_SUBLANES
_VMEM_LIMIT = V7X_VMEM_BYTES - 3 * 1024 * 1024

_STAGE_ROWS, _STAGE_COLS = 256, 1024

_CARRY_STAGE_ORDER = "AABBAABBAABBABBBBABBB"

_SMALL = ("g_mix", "w_conv", "pool_scale", "g_mlp", "g_ple", "g_final")
_BIG = ("w_in", "w_oc", "w_pool", "w_o", "w_up", "w_down", "w_pg", "w_pp")


class _Cfg(NamedTuple):
    seqs: int
    rows: int
    carry: bool
    tiles_per_seq: int
    skew: bool
    pos0: int
    d_ff_chunk: int


def _rms(x, g):
    ms = jnp.mean(x * x, axis=-1, keepdims=True)
    return x * lax.rsqrt(ms + EPS) * g


def _sigmoid(x):
    return 0.5 * jnp.tanh(0.5 * x) + 0.5


def _dot(a, w):
    return jnp.dot(a.astype(jnp.bfloat16), w, preferred_element_type=jnp.float32)


def _stage_chunks(shape):
    lead = range(shape[0]) if len(shape) == 3 else (None,)
    rows, cols = shape[-2:]
    for g in lead:
        for r0 in range(0, rows, _STAGE_ROWS):
            for c0 in range(0, cols, _STAGE_COLS):
                yield (g, slice(r0, min(r0 + _STAGE_ROWS, rows)),
                       slice(c0, min(c0 + _STAGE_COLS, cols)))


def _stage_slots(scratches):
    slots = []
    for ref in scratches:
        assert ref.shape[0] == 1 and ref.shape[2] >= _STAGE_COLS
        for r0 in range(0, ref.shape[1] - _STAGE_ROWS + 1, _STAGE_ROWS):
            slots.append(ref.at[0, r0:r0 + _STAGE_ROWS, 0:_STAGE_COLS])
    return slots


def _round_weights_to_vmem(hbm, vmem, slots, sem):
    n_slots = len(slots)
    jobs = []
    for name in _BIG:
        for g, rs, cs in _stage_chunks(hbm[name].shape):
            src = hbm[name] if g is None else hbm[name].at[g]
            dst = vmem[name] if g is None else vmem[name].at[g]
            jobs.append((src.at[rs, cs], dst, rs, cs))

    def slot_view(n):
        _, _, rs, cs = jobs[n]
        return slots[n % n_slots].at[0:rs.stop - rs.start, 0:cs.stop - cs.start]

    def copy(n):
        return pltpu.make_async_copy(jobs[n][0], slot_view(n), sem.at[n % n_slots])

    for n in range(min(n_slots - 1, len(jobs))):
        copy(n).start()
    for n, (_, dst, rs, cs) in enumerate(jobs):
        if n + n_slots - 1 < len(jobs):
            copy(n + n_slots - 1).start()
        copy(n).wait()
        dst[rs, cs] = slot_view(n)[...].astype(jnp.bfloat16)


def _trunk_kernel(*refs, cfg: _Cfg):
    names = ["x", "p"] + ([] if cfg.carry else ["sc", "sp"]) + list(_SMALL) + list(_BIG)
    names += ["y", "nc", "np"] + (["out_" + n for n in _BIG] if cfg.carry else [])
    names += ["u_ext", "v_ext"]
    if cfg.carry:
        names += (["h_scr"] if cfg.skew else []) + ["vm_" + n for n in _BIG]
        names += ["stage_sem", "out_sem"]
    else:
        names += ["sp_sem", "np_sem"]
    assert len(names) == len(refs)
    r = dict(zip(names, refs))
    x_ref, p_ref, y_ref, nc_ref, np_ref = r["x"], r["p"], r["y"], r["nc"], r["np"]
    u_ext, v_ext = r["u_ext"], r["v_ext"]
    g_mix, w_conv, pool_scale, g_mlp, g_ple, g_final = (r[n] for n in _SMALL)

    S, L = cfg.seqs, cfg.rows
    R = S * L
    D = x_ref.shape[-1]
    C = w_conv.shape[-1]
    n_conv_hist = CONV_W - 1
    n_pool_hist = max(POOL_WINDOWS) - 1
    HU, HV = _CONV_HIST, _POOL_HIST

    if cfg.carry:
        hbm = {n: r[n] for n in _BIG}
        vmem = {n: r["vm_" + n] for n in _BIG}
        step = pl.program_id(0)
        n_steps = pl.num_programs(0)

        def export(i, n):
            return pltpu.make_async_copy(vmem[n], r["out_" + n], r["out_sem"].at[i])

        @pl.when(step == 0)
        def _():
            _round_weights_to_vmem(hbm, vmem, _stage_slots((u_ext, v_ext)), r["stage_sem"])
            for i, n in enumerate(_BIG):
                export(i, n).start()
            if cfg.skew:
                r["h_scr"][...] = jnp.zeros((R, D), jnp.float32)

        w_in, w_oc, w_pool, w_o, w_up, w_down, w_pg, w_pp = (vmem[n] for n in _BIG)
        tile = jnp.minimum(step, n_steps - 2) if cfg.skew else step
        j = lax.rem(tile, cfg.tiles_per_seq)
    else:
        w_in, w_oc, w_pool, w_o, w_up, w_down, w_pg, w_pp = (r[n] for n in _BIG)
        seq0 = pl.program_id(0) * S

        def pool_state_in(i):
            return pltpu.make_async_copy(r["sp"].at[i, pl.ds(seq0, S), :],
                                         v_ext.at[:, HV - n_pool_hist + i, :], r["sp_sem"].at[i])

        def pool_state_out(i):
            return pltpu.make_async_copy(v_ext.at[:, HV + L - n_pool_hist + i, :],
                                         np_ref.at[i, pl.ds(seq0, S), :], r["np_sem"].at[i])

    P = w_pool.shape[0] * w_pool.shape[1]
    GW = w_pool.shape[1]

    def init_history():
        if cfg.carry:
            @pl.when(j == 0)
            def _():
                u_ext[:, 0:HU, :] = jnp.zeros((S, HU, C), jnp.float32)
                v_ext[:, 0:HV, :] = jnp.zeros((S, HV, P), jnp.float32)

            @pl.when(j > 0)
            def _():
                u_ext[:, 0:HU, :] = u_ext[:, L:L + HU, :]
                v_ext[:, 0:HV, :] = v_ext[:, L:L + HV, :]
        else:
            for i in range(n_pool_hist):
                pool_state_in(i).start()
            u_ext[:, HU - n_conv_hist:HU, :] = r["sc"][...]
            v_ext[:, 0:HV - n_pool_hist, :] = jnp.zeros((S, HV - n_pool_hist, P), jnp.float32)

    def write_state():
        if cfg.carry:
            @pl.when(j == cfg.tiles_per_seq - 1)
            def _():
                nc_ref[...] = u_ext[:, HU + L - n_conv_hist:HU + L, :]
                np_ref[...] = v_ext[:, HV + L - n_pool_hist:HV + L, :]
        else:
            nc_ref[...] = u_ext[:, HU + L - n_conv_hist:HU + L, :]
            for i in range(n_pool_hist):
                pool_state_out(i).wait()

    def mixer(load_x, out):
        pos_base = cfg.pos0 + j * L if cfg.carry else cfg.pos0
        xn = _rms(load_x(), g_mix[...]).astype(jnp.bfloat16)
        yield
        o1, o2, o3 = C, 2 * C, 3 * C
        o4 = o3 + P
        o5 = o4 + D
        b = _dot(xn, w_in[:, 0:o1])
        u = _dot(xn, w_in[:, o1:o2]) * _dot(xn, w_in[:, o2:o3])
        u_ext[:, HU:HU + L, :] = u.reshape(S, L, C)
        yield
        conv = None
        for k in range(CONV_W):
            lo = HU - n_conv_hist + k
            term = u_ext[:, lo:lo + L, :].reshape(R, C) * w_conv[k:k + 1, :]
            conv = term if conv is None else conv + term
        ya_in = (b * conv).astype(jnp.bfloat16)
        yield
        ya = _dot(ya_in, w_oc[...])
        v = _dot(xn, w_in[:, o3:o4])
        v_ext[:, HV:HV + L, :] = v.reshape(S, L, P)
        if not cfg.carry:
            for i in range(n_pool_hist):
                pool_state_in(i).wait()
            for i in range(n_pool_hist):
                pool_state_out(i).start()
        yield
        pos = (pos_base + lax.broadcasted_iota(jnp.int32, (S, L, V7X_LANES), 1)
               ).astype(jnp.float32)
        pooled = []
        for gi, w in enumerate(POOL_WINDOWS):
            sl = slice(gi * GW, (gi + 1) * GW)
            s = v_ext[:, :, sl]
            k = 1
            while k < w:
                s = s + pltpu.roll(s, k, axis=1)
                k *= 2
            inv_cnt = 1.0 / jnp.minimum(jnp.float32(w), pos + 1.0)
            inv_cnt = jnp.concatenate([inv_cnt] * (GW // V7X_LANES), axis=-1)
            pooled.append(
                (s[:, HV:HV + L, :] * inv_cnt - v_ext[:, HV:HV + L, sl]
                 ).reshape(R, GW).astype(jnp.bfloat16))
        yield
        yb = jnp.concatenate([_dot(pg, w_pool[gi]) for gi, pg in enumerate(pooled)], axis=-1)
        za = _dot(xn, w_in[:, o4:o5])
        zb = _dot(xn, w_in[:, o5:])
        yield
        mix = (_sigmoid(za) * ya + _sigmoid(zb) * (yb * pool_scale[...])).astype(jnp.bfloat16)
        yield
        out["h"] = load_x() + _dot(mix, w_o[...])

    def mlp_ple(load_h, p, out):
        hn = _rms(load_h(), g_mlp[...]).astype(jnp.bfloat16)
        yield
        d_ff = w_up.shape[1]
        acc = load_h()
        mid = _dot(hn, w_up[:, 0:cfg.d_ff_chunk])
        for c0 in range(0, d_ff, cfg.d_ff_chunk):
            yield
            act = jnp.square(jnp.maximum(mid, 0.0)).astype(jnp.bfloat16)
            yield
            acc = acc + _dot(act, w_down[c0:c0 + cfg.d_ff_chunk, :])
            if c0 + cfg.d_ff_chunk < d_ff:
                mid = _dot(hn, w_up[:, c0 + cfg.d_ff_chunk:c0 + 2 * cfg.d_ff_chunk])
        h = acc
        yield
        hg = _rms(h, g_ple[...]).astype(jnp.bfloat16)
        yield
        zg = _dot(hg, w_pg[...])
        pe = _dot(p, w_pp[...])
        yield
        out["y"] = _rms(h + _sigmoid(zg) * pe, g_final[...])

    def run(order, gens):
        for c in order:
            next(gens[c], None)
        for g in gens.values():
            for _ in g:
                pass

    def load_x():
        return x_ref[...].reshape(R, D)

    p = p_ref[...].reshape(R, p_ref.shape[-1])
    out = {}
    init_history()
    if cfg.skew:
        run(_CARRY_STAGE_ORDER,
            {"A": mixer(load_x, out), "B": mlp_ple(lambda: r["h_scr"][...], p, out)})
        r["h_scr"][...] = out["h"]
    else:
        run("", {"A": mixer(load_x, out)})
        run("", {"B": mlp_ple(lambda: out["h"], p, out)})
    y_ref[...] = out["y"].reshape(S, L, D)
    write_state()

    if cfg.carry:
        @pl.when(step == n_steps - 1)
        def _():
            for i, n in enumerate(_BIG):
                export(i, n).wait()


def _resident(shape):
    zeros = (0,) * len(shape)
    return pl.BlockSpec(shape, lambda *_: zeros, pipeline_mode=pl.Buffered(1))


def _run_group(cfg, n_tiles, x, p, states, small, big):
    S, L = cfg.seqs, cfg.rows
    n_seq, _, D = x.shape
    C = small[1].shape[-1]
    P = big[2].shape[0] * big[2].shape[1]
    n_conv_hist = CONV_W - 1
    n_pool_hist = max(POOL_WINDOWS) - 1
    any_spec = pl.BlockSpec(memory_space=pl.ANY)
    J = cfg.tiles_per_seq

    if cfg.skew:
        grid = (n_tiles + 1,)

        def mix_tile(i):
            t = jnp.minimum(i, n_tiles - 1)
            return (t // J, t % J, 0)

        def mlp_tile(i):
            t = jnp.maximum(i - 1, 0)
            return (t // J, t % J, 0)
    elif cfg.carry:
        grid = (n_tiles,)

        def mix_tile(i):
            return (i // J, i % J, 0)

        mlp_tile = mix_tile
    else:
        grid = (n_tiles,)

        def mix_tile(i):
            return (i, 0, 0)

        mlp_tile = mix_tile

    def seq_map(i):
        s, _, _ = mix_tile(i)
        return (s, 0, 0)

    in_specs = [pl.BlockSpec((S, L, D), mix_tile),
                pl.BlockSpec((S, L, p.shape[-1]), mlp_tile)]
    if states:
        state_conv, state_pool_rows = states
        assert state_pool_rows.shape == (n_pool_hist, n_seq, P)
        in_specs += [pl.BlockSpec((S,) + state_conv.shape[1:], seq_map), any_spec]
    in_specs += [_resident(w.shape) for w in small]
    in_specs += [any_spec if cfg.carry else _resident(w.shape) for w in big]

    out_shape = [jax.ShapeDtypeStruct(x.shape, x.dtype),
                 jax.ShapeDtypeStruct((n_seq, n_conv_hist, C), x.dtype)]
    out_specs = [pl.BlockSpec((S, L, D), mlp_tile),
                 pl.BlockSpec((S, n_conv_hist, C), seq_map)]
    if cfg.carry:
        out_shape += [jax.ShapeDtypeStruct((n_seq, n_pool_hist, P), x.dtype)]
        out_specs += [pl.BlockSpec((S, n_pool_hist, P), seq_map)]
    else:
        out_shape += [jax.ShapeDtypeStruct((n_pool_hist, n_seq, P), x.dtype)]
        out_specs += [any_spec]
    scratch = [pltpu.VMEM((S, _CONV_HIST + L, C), jnp.float32),
               pltpu.VMEM((S, _POOL_HIST + L, P), jnp.float32)]
    if cfg.carry:
        out_shape += [jax.ShapeDtypeStruct(w.shape, jnp.bfloat16) for w in big]
        out_specs += [any_spec for _ in big]
        scratch += [pltpu.VMEM((S * L, D), jnp.float32)] if cfg.skew else []
        scratch += [pltpu.VMEM(w.shape, jnp.bfloat16) for w in big]
        n_stage_slots = (_CONV_HIST + L) // _STAGE_ROWS + (_POOL_HIST + L) // _STAGE_ROWS
        assert S == 1 and n_stage_slots >= 2
        scratch += [pltpu.SemaphoreType.DMA((n_stage_slots,)),
                    pltpu.SemaphoreType.DMA((len(big),))]
    else:
        scratch += [pltpu.SemaphoreType.DMA((n_pool_hist,)),
                    pltpu.SemaphoreType.DMA((n_pool_hist,))]

    outs = pl.pallas_call(
        functools.partial(_trunk_kernel, cfg=cfg),
        grid=grid,
        in_specs=in_specs,
        out_specs=out_specs,
        out_shape=out_shape,
        scratch_shapes=scratch,
        compiler_params=pltpu.CompilerParams(
            dimension_semantics=("arbitrary",),
            vmem_limit_bytes=_VMEM_LIMIT),
        name="trunk_carry" if cfg.carry else "trunk_state",
    )(x, p, *states, *small, *big)
    return outs[:3], tuple(outs[3:])


def kernel(x_prompt, x_sample, state_conv, state_pool, p_prompt, p_sample, g_mix, w_in, w_conv, w_out_conv, w_pool, pool_scale, w_o, g_mlp, w_up, w_down, g_ple, w_ple_gate, w_ple_proj, g_final):
    depth = w_in.shape[0]
    assert depth == 1, "single-layer trunk only"
    small = (g_mix, w_conv[0], pool_scale, g_mlp, g_ple, g_final[None, :])
    big_f32 = (w_in[0], w_out_conv[0], w_pool[0], w_o[0], w_up[0], w_down[0], w_ple_gate[0],
               w_ple_proj[0])

    n_prompt, seq, _ = x_prompt.shape
    prompt_rows = 512
    assert seq % prompt_rows == 0 and prompt_rows >= _POOL_HIST
    tiles_per_seq = seq // prompt_rows
    cfg_p = _Cfg(seqs=1, rows=prompt_rows, carry=True, tiles_per_seq=tiles_per_seq, skew=False,
                 pos0=0, d_ff_chunk=1024)
    (y_prompt, nc_prompt, np_prompt), big_bf16 = _run_group(
        cfg_p, n_prompt * tiles_per_seq, x_prompt, p_prompt[0], (), small, big_f32)

    n_sample, dec_seq, _ = x_sample.shape
    sample_seqs = 32
    assert n_sample % sample_seqs == 0 and dec_seq % V7X_SUBLANES == 0
    cfg_s = _Cfg(seqs=sample_seqs, rows=dec_seq, carry=False, tiles_per_seq=1, skew=False,
                 pos0=PAST_LEN, d_ff_chunk=1024)
    (y_sample, nc_sample, np_sample), _ = _run_group(
        cfg_s, n_sample // sample_seqs, x_sample, p_sample[0],
        (state_conv[0], jnp.swapaxes(state_pool[0], 0, 1)), small, big_bf16)

    return (y_prompt, y_sample, nc_prompt[None], np_prompt[None], nc_sample[None],
            jnp.swapaxes(np_sample, 0, 1)[None])
```

```python
import functools
from typing import NamedTuple

import jax
import jax.numpy as jnp
from jax import lax
from jax.experimental import pallas as pl
from jax.experimental.pallas import tpu as pltpu

CONV_W = 3
POOL_WINDOWS = (2, 4, 8, 16)
PAST_LEN = 16384
EPS = 1e-6

V7X_SUBLANES = 8
V7X_LANES = 128
V7X_VMEM_BYTES = 64 * 1024 * 1024

_CONV_HIST = V7X_SUBLANES
_POOL_HIST = 2 * V7X_SUBLANES
_VMEM_LIMIT = V7X_VMEM_BYTES - 3 * 1024 * 1024

_STAGE_ROWS, _STAGE_COLS = 256, 1024
_STAGE_CHUNKS_PER_STAGE = 5

_SMALL = ("g_mix", "w_conv", "pool_scale", "g_mlp", "g_ple", "g_final")
_MIX_W = ("w_in", "w_oc", "w_pool", "w_o")
_MLP_W = ("w_up", "w_down", "w_pg", "w_pp")
_BIG = _MIX_W + _MLP_W

_CARRY_STAGE_ORDER = "AABBAABBAABBABBBBABBB"
_FIRST_STAGE_ORDER = "CACACACACACACACA"


class _Cfg(NamedTuple):
    seqs: int
    rows: int
    carry: bool
    tiles_per_seq: int
    pos0: int
    d_ff_chunk: int


def _rms(x, g):
    ms = jnp.mean(x * x, axis=-1, keepdims=True)
    return x * lax.rsqrt(ms + EPS) * g


def _sigmoid(x):
    return 0.5 * jnp.tanh(0.5 * x) + 0.5


def _dot(a, w):
    return jnp.dot(a.astype(jnp.bfloat16), w, preferred_element_type=jnp.float32)


def _run_stages(order, gens):
    for c in order:
        next(gens[c], None)
    for g in gens.values():
        for _ in g:
            pass


def _stage_chunks(shape):
    lead = range(shape[0]) if len(shape) == 3 else (None,)
    rows, cols = shape[-2:]
    for g in lead:
        for r0 in range(0, rows, _STAGE_ROWS):
            for c0 in range(0, cols, _STAGE_COLS):
                yield (g, slice(r0, min(r0 + _STAGE_ROWS, rows)),
                       slice(c0, min(c0 + _STAGE_COLS, cols)))


def _stage_slots(refs):
    slots = []
    for ref in refs:
        view = ref.at[0] if len(ref.shape) == 3 else ref
        assert view.shape[1] >= _STAGE_COLS
        for r0 in range(0, view.shape[0] - _STAGE_ROWS + 1, _STAGE_ROWS):
            slots.append(view.at[r0:r0 + _STAGE_ROWS, 0:_STAGE_COLS])
    return slots


def _round_weights(names, hbm, vmem, slots, sem, chunks_per_stage=None):
    n_slots = len(slots)
    assert n_slots >= 2
    jobs = []
    for name in names:
        for g, rs, cs in _stage_chunks(hbm[name].shape):
            src = hbm[name] if g is None else hbm[name].at[g]
            dst = vmem[name] if g is None else vmem[name].at[g]
            jobs.append((src.at[rs, cs], dst, rs, cs))

    def slot_view(n):
        _, _, rs, cs = jobs[n]
        return slots[n % n_slots].at[0:rs.stop - rs.start, 0:cs.stop - cs.start]

    def copy(n):
        return pltpu.make_async_copy(jobs[n][0], slot_view(n), sem.at[n % n_slots])

    for n in range(min(n_slots - 1, len(jobs))):
        copy(n).start()
    for n, (_, dst, rs, cs) in enumerate(jobs):
        if n + n_slots - 1 < len(jobs):
            copy(n + n_slots - 1).start()
        copy(n).wait()
        dst[rs, cs] = slot_view(n)[...].astype(jnp.bfloat16)
        if chunks_per_stage and (n + 1) % chunks_per_stage == 0:
            yield


def _trunk_kernel(*refs, cfg: _Cfg):
    names = ["x", "p"] + ([] if cfg.carry else ["sc", "sp"]) + list(_SMALL) + list(_BIG)
    names += ["y", "nc", "np"] + (["out_" + n for n in _BIG] if cfg.carry else [])
    names += ["u_ext", "v_ext"]
    if cfg.carry:
        names += ["h_scr"] + ["vm_" + n for n in _BIG] + ["stage_sem", "out_sem"]
    else:
        names += ["sp_sem", "np_sem"]
    assert len(names) == len(refs)
    r = dict(zip(names, refs))
    x_ref, p_ref, y_ref, nc_ref, np_ref = r["x"], r["p"], r["y"], r["nc"], r["np"]
    u_ext, v_ext = r["u_ext"], r["v_ext"]
    g_mix, w_conv, pool_scale, g_mlp, g_ple, g_final = (r[n] for n in _SMALL)

    S, L = cfg.seqs, cfg.rows
    R = S * L
    D = x_ref.shape[-1]
    C = w_conv.shape[-1]
    n_conv_hist = CONV_W - 1
    n_pool_hist = max(POOL_WINDOWS) - 1
    HU, HV = _CONV_HIST, _POOL_HIST

    if cfg.carry:
        hbm = {n: r[n] for n in _BIG}
        vmem = {n: r["vm_" + n] for n in _BIG}
        h_scr = r["h_scr"]
        step = pl.program_id(0)
        n_steps = pl.num_programs(0)
        w_in, w_oc, w_pool, w_o, w_up, w_down, w_pg, w_pp = (vmem[n] for n in _BIG)
        j = lax.rem(jnp.minimum(step, n_steps - 2), cfg.tiles_per_seq)

        def export(i, n):
            return pltpu.make_async_copy(vmem[n], r["out_" + n], r["out_sem"].at[i])
    else:
        w_in, w_oc, w_pool, w_o, w_up, w_down, w_pg, w_pp = (r[n] for n in _BIG)
        seq0 = pl.program_id(0) * S

        def pool_state_in(i):
            return pltpu.make_async_copy(r["sp"].at[i, pl.ds(seq0, S), :],
                                         v_ext.at[:, HV - n_pool_hist + i, :], r["sp_sem"].at[i])

        def pool_state_out(i):
            return pltpu.make_async_copy(v_ext.at[:, HV + L - n_pool_hist + i, :],
                                         np_ref.at[i, pl.ds(seq0, S), :], r["np_sem"].at[i])

    P = w_pool.shape[0] * w_pool.shape[1]
    GW = w_pool.shape[1]

    def init_history():
        if cfg.carry:
            @pl.when(j == 0)
            def _():
                u_ext[:, 0:HU, :] = jnp.zeros((S, HU, C), jnp.float32)
                v_ext[:, 0:HV, :] = jnp.zeros((S, HV, P), jnp.float32)

            @pl.when(j > 0)
            def _():
                u_ext[:, 0:HU, :] = u_ext[:, L:L + HU, :]
                v_ext[:, 0:HV, :] = v_ext[:, L:L + HV, :]
        else:
            for i in range(n_pool_hist):
                pool_state_in(i).start()
            u_ext[:, HU - n_conv_hist:HU, :] = r["sc"][...]
            v_ext[:, 0:HV - n_pool_hist, :] = jnp.zeros((S, HV - n_pool_hist, P), jnp.float32)

    def write_state():
        if cfg.carry:
            @pl.when(j == cfg.tiles_per_seq - 1)
            def _():
                nc_ref[...] = u_ext[:, HU + L - n_conv_hist:HU + L, :]
                np_ref[...] = v_ext[:, HV + L - n_pool_hist:HV + L, :]
        else:
            nc_ref[...] = u_ext[:, HU + L - n_conv_hist:HU + L, :]
            for i in range(n_pool_hist):
                pool_state_out(i).wait()

    def mixer(load_x, out):
        pos_base = cfg.pos0 + j * L if cfg.carry else cfg.pos0
        xn = _rms(load_x(), g_mix[...]).astype(jnp.bfloat16)
        yield
        o1, o2, o3 = C, 2 * C, 3 * C
        o4 = o3 + P
        o5 = o4 + D
        b = _dot(xn, w_in[:, 0:o1])
        u = _dot(xn, w_in[:, o1:o2]) * _dot(xn, w_in[:, o2:o3])
        u_ext[:, HU:HU + L, :] = u.reshape(S, L, C)
        yield
        conv = None
        for k in range(CONV_W):
            lo = HU - n_conv_hist + k
            term = u_ext[:, lo:lo + L, :].reshape(R, C) * w_conv[k:k + 1, :]
            conv = term if conv is None else conv + term
        ya_in = (b * conv).astype(jnp.bfloat16)
        yield
        ya = _dot(ya_in, w_oc[...])
        v = _dot(xn, w_in[:, o3:o4])
        v_ext[:, HV:HV + L, :] = v.reshape(S, L, P)
        if not cfg.carry:
            for i in range(n_pool_hist):
                pool_state_in(i).wait()
            for i in range(n_pool_hist):
                pool_state_out(i).start()
        yield
        pos = (pos_base + lax.broadcasted_iota(jnp.int32, (S, L, V7X_LANES), 1)
               ).astype(jnp.float32)
        pooled = []
        for gi, w in enumerate(POOL_WINDOWS):
            sl = slice(gi * GW, (gi + 1) * GW)
            s = v_ext[:, :, sl]
            k = 1
            while k < w:
                s = s + pltpu.roll(s, k, axis=1)
                k *= 2
            inv_cnt = 1.0 / jnp.minimum(jnp.float32(w), pos + 1.0)
            inv_cnt = jnp.concatenate([inv_cnt] * (GW // V7X_LANES), axis=-1)
            pooled.append(
                (s[:, HV:HV + L, :] * inv_cnt - v_ext[:, HV:HV + L, sl]
                 ).reshape(R, GW).astype(jnp.bfloat16))
        yield
        yb = jnp.concatenate([_dot(pg, w_pool[gi]) for gi, pg in enumerate(pooled)], axis=-1)
        za = _dot(xn, w_in[:, o4:o5])
        zb = _dot(xn, w_in[:, o5:])
        yield
        mix = (_sigmoid(za) * ya + _sigmoid(zb) * (yb * pool_scale[...])).astype(jnp.bfloat16)
        yield
        out["h"] = load_x() + _dot(mix, w_o[...])

    def mlp_ple(load_h, out):
        hn = _rms(load_h(), g_mlp[...]).astype(jnp.bfloat16)
        yield
        d_ff = w_up.shape[1]
        acc = load_h()
        mid = _dot(hn, w_up[:, 0:cfg.d_ff_chunk])
        for c0 in range(0, d_ff, cfg.d_ff_chunk):
            yield
            act = jnp.square(jnp.maximum(mid, 0.0)).astype(jnp.bfloat16)
            yield
            acc = acc + _dot(act, w_down[c0:c0 + cfg.d_ff_chunk, :])
            if c0 + cfg.d_ff_chunk < d_ff:
                mid = _dot(hn, w_up[:, c0 + cfg.d_ff_chunk:c0 + 2 * cfg.d_ff_chunk])
        h = acc
        yield
        hg = _rms(h, g_ple[...]).astype(jnp.bfloat16)
        yield
        zg = _dot(hg, w_pg[...])
        pe = _dot(p_ref[...].reshape(R, p_ref.shape[-1]), w_pp[...])
        yield
        out["y"] = _rms(h + _sigmoid(zg) * pe, g_final[...])

    def load_x():
        return x_ref[...].reshape(R, D)

    def load_h_prev():
        return h_scr[...]

    if not cfg.carry:
        out = {}
        init_history()
        _run_stages("", {"A": mixer(load_x, out)})
        _run_stages("", {"B": mlp_ple(lambda: out["h"], out)})
        y_ref[...] = out["y"].reshape(S, L, D)
        write_state()
        return

    @pl.when(step == 0)
    def _():
        idle = _stage_slots((h_scr, y_ref))
        for _ in _round_weights(_MIX_W, hbm, vmem, _stage_slots((u_ext, v_ext)) + idle,
                                r["stage_sem"]):
            pass
        out = {}
        init_history()
        _run_stages(_FIRST_STAGE_ORDER,
                    {"A": mixer(load_x, out),
                     "C": _round_weights(_MLP_W, hbm, vmem, idle, r["stage_sem"],
                                         _STAGE_CHUNKS_PER_STAGE)})
        h_scr[...] = out["h"]
        write_state()
        for i, n in enumerate(_BIG):
            export(i, n).start()

    @pl.when(jnp.logical_and(step > 0, step < n_steps - 1))
    def _():
        out = {}
        init_history()
        _run_stages(_CARRY_STAGE_ORDER,
                    {"A": mixer(load_x, out), "B": mlp_ple(load_h_prev, out)})
        y_ref[...] = out["y"].reshape(S, L, D)
        h_scr[...] = out["h"]
        write_state()

    @pl.when(step == n_steps - 1)
    def _():
        out = {}
        _run_stages("", {"B": mlp_ple(load_h_prev, out)})
        y_ref[...] = out["y"].reshape(S, L, D)
        for i, n in enumerate(_BIG):
            export(i, n).wait()


def _resident(shape):
    zeros = (0,) * len(shape)
    return pl.BlockSpec(shape, lambda *_: zeros, pipeline_mode=pl.Buffered(1))


def _run_group(cfg, n_tiles, x, p, states, small, big):
    S, L = cfg.seqs, cfg.rows
    n_seq, _, D = x.shape
    C = small[1].shape[-1]
    P = big[2].shape[0] * big[2].shape[1]
    n_conv_hist = CONV_W - 1
    n_pool_hist = max(POOL_WINDOWS) - 1
    any_spec = pl.BlockSpec(memory_space=pl.ANY)
    J = cfg.tiles_per_seq

    if cfg.carry:
        grid = (n_tiles + 1,)

        def mix_tile(i):
            t = jnp.minimum(i, n_tiles - 1)
            return (t // J, t % J, 0)

        def mlp_tile(i):
            t = jnp.maximum(i - 1, 0)
            return (t // J, t % J, 0)
    else:
        grid = (n_tiles,)

        def mix_tile(i):
            return (i, 0, 0)

        mlp_tile = mix_tile

    def seq_map(i):
        s, _, _ = mix_tile(i)
        return (s, 0, 0)

    in_specs = [pl.BlockSpec((S, L, D), mix_tile),
                pl.BlockSpec((S, L, p.shape[-1]), mlp_tile)]
    if states:
        state_conv, state_pool_rows = states
        assert state_pool_rows.shape == (n_pool_hist, n_seq, P)
        in_specs += [pl.BlockSpec((S,) + state_conv.shape[1:], seq_map), any_spec]
    in_specs += [_resident(w.shape) for w in small]
    in_specs += [any_spec if cfg.carry else _resident(w.shape) for w in big]

    out_shape = [jax.ShapeDtypeStruct(x.shape, x.dtype),
                 jax.ShapeDtypeStruct((n_seq, n_conv_hist, C), x.dtype)]
    out_specs = [pl.BlockSpec((S, L, D), mlp_tile),
                 pl.BlockSpec((S, n_conv_hist, C), seq_map)]
    if cfg.carry:
        out_shape += [jax.ShapeDtypeStruct((n_seq, n_pool_hist, P), x.dtype)]
        out_specs += [pl.BlockSpec((S, n_pool_hist, P), seq_map)]
    else:
        out_shape += [jax.ShapeDtypeStruct((n_pool_hist, n_seq, P), x.dtype)]
        out_specs += [any_spec]
    scratch = [pltpu.VMEM((S, _CONV_HIST + L, C), jnp.float32),
               pltpu.VMEM((S, _POOL_HIST + L, P), jnp.float32)]
    if cfg.carry:
        out_shape += [jax.ShapeDtypeStruct(w.shape, jnp.bfloat16) for w in big]
        out_specs += [any_spec for _ in big]
        scratch += [pltpu.VMEM((S * L, D), jnp.float32)]
        scratch += [pltpu.VMEM(w.shape, jnp.bfloat16) for w in big]
        assert S == 1 and L % _STAGE_ROWS == 0
        n_stage_slots = 4 * (L // _STAGE_ROWS)
        scratch += [pltpu.SemaphoreType.DMA((n_stage_slots,)),
                    pltpu.SemaphoreType.DMA((len(big),))]
    else:
        scratch += [pltpu.SemaphoreType.DMA((n_pool_hist,)),
                    pltpu.SemaphoreType.DMA((n_pool_hist,))]

    outs = pl.pallas_call(
        functools.partial(_trunk_kernel, cfg=cfg),
        grid=grid,
        in_specs=in_specs,
        out_specs=out_specs,
        out_shape=out_shape,
        scratch_shapes=scratch,
        compiler_params=pltpu.CompilerParams(
            dimension_semantics=("arbitrary",),
            vmem_limit_bytes=_VMEM_LIMIT),
        name="trunk_carry" if cfg.carry else "trunk_state",
    )(x, p, *states, *small, *big)
    return outs[:3], tuple(outs[3:])


def kernel(x_prompt, x_sample, state_conv, state_pool, p_prompt, p_sample, g_mix, w_in, w_conv, w_out_conv, w_pool, pool_scale, w_o, g_mlp, w_up, w_down, g_ple, w_ple_gate, w_ple_proj, g_final):
    depth = w_in.shape[0]
    assert depth == 1, "single-layer trunk only"
    small = (g_mix, w_conv[0], pool_scale, g_mlp, g_ple, g_final[None, :])
    big_f32 = (w_in[0], w_out_conv[0], w_pool[0], w_o[0], w_up[0], w_down[0], w_ple_gate[0],
               w_ple_proj[0])

    n_prompt, seq, _ = x_prompt.shape
    prompt_rows = 512
    assert seq % prompt_rows == 0 and prompt_rows >= _POOL_HIST
    tiles_per_seq = seq // prompt_rows
    cfg_p = _Cfg(seqs=1, rows=prompt_rows, carry=True, tiles_per_seq=tiles_per_seq, pos0=0,
                 d_ff_chunk=1024)
    (y_prompt, nc_prompt, np_prompt), big_bf16 = _run_group(
        cfg_p, n_prompt * tiles_per_seq, x_prompt, p_prompt[0], (), small, big_f32)

    n_sample, dec_seq, _ = x_sample.shape
    sample_seqs = 32
    assert n_sample % sample_seqs == 0 and dec_seq % V7X_SUBLANES == 0
    cfg_s = _Cfg(seqs=sample_seqs, rows=dec_seq, carry=False, tiles_per_seq=1, pos0=PAST_LEN,
                 d_ff_chunk=1024)
    (y_sample, nc_sample, np_sample), _ = _run_group(
        cfg_s, n_sample // sample_seqs, x_sample, p_sample[0],
        (state_conv[0], jnp.swapaxes(state_pool[0], 0, 1)), small, big_bf16)

    return (y_prompt, y_sample, nc_prompt[None], np_prompt[None], nc_sample[None],
            jnp.swapaxes(np_sample, 0, 1)[None])
```

```python
import functools
from typing import NamedTuple

import jax
import jax.numpy as jnp
from jax import lax
from jax.experimental import pallas as pl
from jax.experimental.pallas import tpu as pltpu

CONV_W = 3
POOL_WINDOWS = (2, 4, 8, 16)
PAST_LEN = 16384
EPS = 1e-6

V7X_SUBLANES = 8
V7X_LANES = 128
V7X_VMEM_BYTES = 64 * 1024 * 1024

_CONV_HIST = V7X_SUBLANES
_POOL_HIST = 2 * V7X_SUBLANES
_VMEM_LIMIT = V7X_VMEM_BYTES - 3 * 1024 * 1024

_STAGE_ROWS, _STAGE_COLS = 256, 1024

_SMALL = ("g_mix", "w_conv", "pool_scale", "g_mlp", "g_ple", "g_final")
_BIG = ("w_in", "w_oc", "w_pool", "w_o", "w_up", "w_down", "w_pg", "w_pp")


class _Cfg(NamedTuple):
    seqs: int
    rows: int
    carry: bool
    tiles_per_seq: int
    pos0: int
    d_ff_chunk: int


def _rms(x, g):
    ms = jnp.mean(x * x, axis=-1, keepdims=True)
    return x * lax.rsqrt(ms + EPS) * g


def _sigmoid(x):
    return 0.5 * jnp.tanh(0.5 * x) + 0.5


def _dot(a, w):
    return jnp.dot(a.astype(jnp.bfloat16), w, preferred_element_type=jnp.float32)


def _stage_chunks(shape):
    lead = range(shape[0]) if len(shape) == 3 else (None,)
    rows, cols = shape[-2:]
    for g in lead:
        for r0 in range(0, rows, _STAGE_ROWS):
            for c0 in range(0, cols, _STAGE_COLS):
                yield (g, slice(r0, min(r0 + _STAGE_ROWS, rows)),
                       slice(c0, min(c0 + _STAGE_COLS, cols)))


def _stage_slots(scratches):
    slots = []
    for ref in scratches:
        assert ref.shape[0] == 1 and ref.shape[2] >= _STAGE_COLS
        for r0 in range(0, ref.shape[1] - _STAGE_ROWS + 1, _STAGE_ROWS):
            slots.append(ref.at[0, r0:r0 + _STAGE_ROWS, 0:_STAGE_COLS])
    return slots


def _round_weights_to_vmem(hbm, vmem, slots, sem):
    n_slots = len(slots)
    jobs = []
    for name in _BIG:
        for g, rs, cs in _stage_chunks(hbm[name].shape):
            src = hbm[name] if g is None else hbm[name].at[g]
            dst = vmem[name] if g is None else vmem[name].at[g]
            jobs.append((src.at[rs, cs], dst, rs, cs))

    def slot_view(n):
        _, _, rs, cs = jobs[n]
        return slots[n % n_slots].at[0:rs.stop - rs.start, 0:cs.stop - cs.start]

    def copy(n):
        return pltpu.make_async_copy(jobs[n][0], slot_view(n), sem.at[n % n_slots])

    for n in range(min(n_slots - 1, len(jobs))):
        copy(n).start()
    for n, (_, dst, rs, cs) in enumerate(jobs):
        if n + n_slots - 1 < len(jobs):
            copy(n + n_slots - 1).start()
        copy(n).wait()
        dst[rs, cs] = slot_view(n)[...].astype(jnp.bfloat16)


def _trunk_kernel(*refs, cfg: _Cfg):
    names = ["x", "p"] + ([] if cfg.carry else ["sc", "sp"]) + list(_SMALL) + list(_BIG)
    names += ["y", "nc", "np"] + (["out_" + n for n in _BIG] if cfg.carry else [])
    names += ["u_ext", "v_ext"]
    if cfg.carry:
        names += ["vm_" + n for n in _BIG]
        names += ["stage_sem", "out_sem"]
    else:
        names += ["sp_sem", "np_sem"]
    assert len(names) == len(refs)
    r = dict(zip(names, refs))
    x_ref, p_ref, y_ref, nc_ref, np_ref = r["x"], r["p"], r["y"], r["nc"], r["np"]
    u_ext, v_ext = r["u_ext"], r["v_ext"]
    g_mix, w_conv, pool_scale, g_mlp, g_ple, g_final = (r[n] for n in _SMALL)

    S, L = cfg.seqs, cfg.rows
    R = S * L
    D = x_ref.shape[-1]
    C = w_conv.shape[-1]
    n_conv_hist = CONV_W - 1
    n_pool_hist = max(POOL_WINDOWS) - 1
    HU, HV = _CONV_HIST, _POOL_HIST

    if cfg.carry:
        hbm = {n: r[n] for n in _BIG}
        vmem = {n: r["vm_" + n] for n in _BIG}
        step = pl.program_id(0)
        n_steps = pl.num_programs(0)

        def export(i, n):
            return pltpu.make_async_copy(vmem[n], r["out_" + n], r["out_sem"].at[i])

        @pl.when(step == 0)
        def _():
            _round_weights_to_vmem(hbm, vmem, _stage_slots((u_ext, v_ext)), r["stage_sem"])
            for i, n in enumerate(_BIG):
                export(i, n).start()

        w_in, w_oc, w_pool, w_o, w_up, w_down, w_pg, w_pp = (vmem[n] for n in _BIG)
        j = lax.rem(step, cfg.tiles_per_seq)
    else:
        w_in, w_oc, w_pool, w_o, w_up, w_down, w_pg, w_pp = (r[n] for n in _BIG)
        seq0 = pl.program_id(0) * S

        def pool_state_in(i):
            return pltpu.make_async_copy(r["sp"].at[i, pl.ds(seq0, S), :],
                                         v_ext.at[:, HV - n_pool_hist + i, :], r["sp_sem"].at[i])

        def pool_state_out(i):
            return pltpu.make_async_copy(v_ext.at[:, HV + L - n_pool_hist + i, :],
                                         np_ref.at[i, pl.ds(seq0, S), :], r["np_sem"].at[i])

    P = w_pool.shape[0] * w_pool.shape[1]
    GW = w_pool.shape[1]

    def init_history():
        if cfg.carry:
            @pl.when(j == 0)
            def _():
                u_ext[:, 0:HU, :] = jnp.zeros((S, HU, C), jnp.float32)
                v_ext[:, 0:HV, :] = jnp.zeros((S, HV, P), jnp.float32)

            @pl.when(j > 0)
            def _():
                u_ext[:, 0:HU, :] = u_ext[:, L:L + HU, :]
                v_ext[:, 0:HV, :] = v_ext[:, L:L + HV, :]
        else:
            for i in range(n_pool_hist):
                pool_state_in(i).start()
            u_ext[:, HU - n_conv_hist:HU, :] = r["sc"][...]
            v_ext[:, 0:HV - n_pool_hist, :] = jnp.zeros((S, HV - n_pool_hist, P), jnp.float32)

    def write_state():
        if cfg.carry:
            @pl.when(j == cfg.tiles_per_seq - 1)
            def _():
                nc_ref[...] = u_ext[:, HU + L - n_conv_hist:HU + L, :]
                np_ref[...] = v_ext[:, HV + L - n_pool_hist:HV + L, :]
        else:
            nc_ref[...] = u_ext[:, HU + L - n_conv_hist:HU + L, :]
            for i in range(n_pool_hist):
                pool_state_out(i).wait()

    def load_x():
        return x_ref[...].reshape(R, D)

    init_history()
    pos_base = cfg.pos0 + j * L if cfg.carry else cfg.pos0

    pe = _dot(p_ref[...].reshape(R, p_ref.shape[-1]), w_pp[...])
    xn = _rms(load_x(), g_mix[...]).astype(jnp.bfloat16)

    o1, o2, o3 = C, 2 * C, 3 * C
    o4 = o3 + P
    o5 = o4 + D
    u = _dot(xn, w_in[:, o1:o2]) * _dot(xn, w_in[:, o2:o3])
    u_ext[:, HU:HU + L, :] = u.reshape(S, L, C)
    v = _dot(xn, w_in[:, o3:o4])
    v_ext[:, HV:HV + L, :] = v.reshape(S, L, P)
    if not cfg.carry:
        for i in range(n_pool_hist):
            pool_state_in(i).wait()
        for i in range(n_pool_hist):
            pool_state_out(i).start()
    b = _dot(xn, w_in[:, 0:o1])
    za = _dot(xn, w_in[:, o4:o5])

    conv = None
    for k in range(CONV_W):
        lo = HU - n_conv_hist + k
        term = u_ext[:, lo:lo + L, :].reshape(R, C) * w_conv[0, k:k + 1, :]
        conv = term if conv is None else conv + term
    ya = _dot(b * conv, w_oc[...])

    pos = (pos_base + lax.broadcasted_iota(jnp.int32, (S, L, V7X_LANES), 1)).astype(jnp.float32)
    pooled = []
    for gi, w in enumerate(POOL_WINDOWS):
        sl = slice(gi * GW, (gi + 1) * GW)
        s = v_ext[:, :, sl]
        k = 1
        while k < w:
            s = s + pltpu.roll(s, k, axis=1)
            k *= 2
        inv_cnt = 1.0 / jnp.minimum(jnp.float32(w), pos + 1.0)
        inv_cnt = jnp.concatenate([inv_cnt] * (GW // V7X_LANES), axis=-1)
        pooled.append((s[:, HV:HV + L, :] * inv_cnt - v_ext[:, HV:HV + L, sl]
                       ).reshape(R, GW).astype(jnp.bfloat16))
    gated_a = _sigmoid(za) * ya
    yb = jnp.concatenate([_dot(pg, w_pool[gi]) for gi, pg in enumerate(pooled)], axis=-1)
    zb = _dot(xn, w_in[:, o5:])
    h = load_x() + _dot(gated_a + _sigmoid(zb) * (yb * pool_scale[...]), w_o[...])

    hn = _rms(h, g_mlp[...]).astype(jnp.bfloat16)
    d_ff = w_up.shape[1]
    acc = h
    mid = _dot(hn, w_up[:, 0:cfg.d_ff_chunk])
    for c0 in range(0, d_ff, cfg.d_ff_chunk):
        nxt = c0 + cfg.d_ff_chunk
        mid_next = _dot(hn, w_up[:, nxt:nxt + cfg.d_ff_chunk]) if nxt < d_ff else None
        act = jnp.square(jnp.maximum(mid, 0.0))
        acc = acc + _dot(act, w_down[c0:nxt, :])
        mid = mid_next
    h = acc

    zg = _dot(_rms(h, g_ple[...]), w_pg[...])
    y_ref[...] = _rms(h + _sigmoid(zg) * pe, g_final[...]).reshape(S, L, D)
    write_state()

    if cfg.carry:
        @pl.when(step == n_steps - 1)
        def _():
            for i, n in enumerate(_BIG):
                export(i, n).wait()


def _resident(shape):
    zeros = (0,) * len(shape)
    return pl.BlockSpec(shape, lambda *_: zeros, pipeline_mode=pl.Buffered(1))


def _run_group(cfg, n_tiles, x, p, states, small, big):
    S, L = cfg.seqs, cfg.rows
    n_seq, _, D = x.shape
    C = small[1].shape[-1]
    P = big[2].shape[0] * big[2].shape[1]
    n_conv_hist = CONV_W - 1
    n_pool_hist = max(POOL_WINDOWS) - 1
    any_spec = pl.BlockSpec(memory_space=pl.ANY)
    J = cfg.tiles_per_seq

    grid = (n_tiles,)
    if cfg.carry:
        def tile_map(i):
            return (i // J, i % J, 0)
    else:
        def tile_map(i):
            return (i, 0, 0)

    def seq_map(i):
        s, _, _ = tile_map(i)
        return (s, 0, 0)

    in_specs = [pl.BlockSpec((S, L, D), tile_map),
                pl.BlockSpec((S, L, p.shape[-1]), tile_map)]
    if states:
        state_conv, state_pool_rows = states
        assert state_pool_rows.shape == (n_pool_hist, n_seq, P)
        in_specs += [pl.BlockSpec((S,) + state_conv.shape[1:], seq_map), any_spec]
    in_specs += [_resident(w.shape) for w in small]
    in_specs += [any_spec if cfg.carry else _resident(w.shape) for w in big]

    out_shape = [jax.ShapeDtypeStruct(x.shape, x.dtype),
                 jax.ShapeDtypeStruct((n_seq, n_conv_hist, C), x.dtype)]
    out_specs = [pl.BlockSpec((S, L, D), tile_map),
                 pl.BlockSpec((S, n_conv_hist, C), seq_map)]
    if cfg.carry:
        out_shape += [jax.ShapeDtypeStruct((n_seq, n_pool_hist, P), x.dtype)]
        out_specs += [pl.BlockSpec((S, n_pool_hist, P), seq_map)]
    else:
        out_shape += [jax.ShapeDtypeStruct((n_pool_hist, n_seq, P), x.dtype)]
        out_specs += [any_spec]
    scratch = [pltpu.VMEM((S, _CONV_HIST + L, C), jnp.float32),
               pltpu.VMEM((S, _POOL_HIST + L, P), jnp.float32)]
    if cfg.carry:
        out_shape += [jax.ShapeDtypeStruct(w.shape, jnp.bfloat16) for w in big]
        out_specs += [any_spec for _ in big]
        scratch += [pltpu.VMEM(w.shape, jnp.bfloat16) for w in big]
        n_stage_slots = (_CONV_HIST + L) // _STAGE_ROWS + (_POOL_HIST + L) // _STAGE_ROWS
        assert S == 1 and n_stage_slots >= 2
        scratch += [pltpu.SemaphoreType.DMA((n_stage_slots,)),
                    pltpu.SemaphoreType.DMA((len(big),))]
    else:
        scratch += [pltpu.SemaphoreType.DMA((n_pool_hist,)),
                    pltpu.SemaphoreType.DMA((n_pool_hist,))]

    outs = pl.pallas_call(
        functools.partial(_trunk_kernel, cfg=cfg),
        grid=grid,
        in_specs=in_specs,
        out_specs=out_specs,
        out_shape=out_shape,
        scratch_shapes=scratch,
        compiler_params=pltpu.CompilerParams(
            dimension_semantics=("arbitrary",),
            vmem_limit_bytes=_VMEM_LIMIT),
        name="trunk_carry" if cfg.carry else "trunk_state",
    )(x, p, *states, *small, *big)
    return outs[:3], tuple(outs[3:])


def kernel(x_prompt, x_sample, state_conv, state_pool, p_prompt, p_sample, g_mix, w_in, w_conv, w_out_conv, w_pool, pool_scale, w_o, g_mlp, w_up, w_down, g_ple, w_ple_gate, w_ple_proj, g_final):
    depth = w_in.shape[0]
    assert depth == 1, "single-layer trunk only"
    small = (g_mix, w_conv, pool_scale, g_mlp, g_ple, g_final[None, :])
    big_f32 = (w_in[0], w_out_conv[0], w_pool[0], w_o[0], w_up[0], w_down[0], w_ple_gate[0],
               w_ple_proj[0])

    n_prompt, seq, _ = x_prompt.shape
    prompt_rows = 512
    assert seq % prompt_rows == 0 and prompt_rows >= _POOL_HIST
    tiles_per_seq = seq // prompt_rows
    cfg_p = _Cfg(seqs=1, rows=prompt_rows, carry=True, tiles_per_seq=tiles_per_seq,
                 pos0=0, d_ff_chunk=1024)
    (y_prompt, nc_prompt, np_prompt), big_bf16 = _run_group(
        cfg_p, n_prompt * tiles_per_seq, x_prompt, p_prompt[0], (), small, big_f32)

    n_sample, dec_seq, _ = x_sample.shape
    sample_seqs = 32
    assert n_sample % sample_seqs == 0 and dec_seq % V7X_SUBLANES == 0
    cfg_s = _Cfg(seqs=sample_seqs, rows=dec_seq, carry=False, tiles_per_seq=1,
                 pos0=PAST_LEN, d_ff_chunk=1024)
    (y_sample, nc_sample, np_sample), _ = _run_group(
        cfg_s, n_sample // sample_seqs, x_sample, p_sample[0],
        (state_conv[0], jnp.swapaxes(state_pool[0], 0, 1)), small, big_bf16)

    return (y_prompt, y_sample, nc_prompt[None], np_prompt[None], nc_sample[None],
            jnp.swapaxes(np_sample, 0, 1)[None])
```

```python
import functools
from typing import NamedTuple

import jax
import jax.numpy as jnp
from jax import lax
from jax.experimental import pallas as pl
from jax.experimental.pallas import tpu as pltpu

CONV_W = 3
POOL_WINDOWS = (2, 4, 8, 16)
PAST_LEN = 16384
EPS = 1e-6
N_CONV_HIST = CONV_W - 1
N_POOL_HIST = max(POOL_WINDOWS) - 1

V7X_SUBLANES = 8
V7X_LANES = 128
V7X_VMEM_BYTES = 64 * 1024 * 1024

_CONV_HIST = V7X_SUBLANES
_POOL_HIST = 2 * V7X_SUBLANES
_VMEM_LIMIT = V7X_VMEM_BYTES - 3 * 1024 * 1024

_STAGE_ROWS, _STAGE_COLS = 256, 1024

_SMALL = ("g_mix", "w_conv", "pool_scale", "g_mlp", "g_ple", "g_final")
_BIG = ("w_in", "w_oc", "w_pool", "w_o", "w_up", "w_down", "w_pg", "w_pp")


class _Cfg(NamedTuple):
    prompt_rows: int
    tiles_per_seq: int
    sample_seqs: int
    sample_rows: int
    d_ff_chunk: int


def _rms(x, g):
    ms = jnp.mean(x * x, axis=-1, keepdims=True)
    return x * lax.rsqrt(ms + EPS) * g


def _sigmoid(x):
    return 0.5 * jnp.tanh(0.5 * x) + 0.5


def _dot(a, w):
    return jnp.dot(a.astype(jnp.bfloat16), w, preferred_element_type=jnp.float32)


def _round_weights_to_vmem(hbm, vmem, slot_pairs, sem):
    pieces = []
    for name in _BIG:
        shape = hbm[name].shape
        for g in (range(shape[0]) if len(shape) == 3 else (None,)):
            src = hbm[name] if g is None else hbm[name].at[g]
            dst = vmem[name] if g is None else vmem[name].at[g]
            rows, cols = shape[-2:]
            cr, cc = min(rows, _STAGE_ROWS), min(cols, _STAGE_COLS)
            assert rows % cr == 0 and cols % cc == 0
            pieces.append((src, dst, cr, cc, cols // cc, (rows // cr) * (cols // cc)))

    def offsets(k, n, slot):
        _, _, cr, cc, per_row, _ = pieces[k]
        if isinstance(n, int):
            return (n // per_row) * cr, (n % per_row) * cc, slot * _STAGE_ROWS
        return (pl.multiple_of(lax.div(n, per_row) * cr, cr),
                pl.multiple_of(lax.rem(n, per_row) * cc, cc),
                pl.multiple_of(slot * _STAGE_ROWS, _STAGE_ROWS))

    def chunk_copy(k, n, slot):
        src, _, cr, cc, _, _ = pieces[k]
        r0, c0, row0 = offsets(k, n, slot)
        return pltpu.make_async_copy(
            src.at[pl.ds(r0, cr), pl.ds(c0, cc)],
            slot_pairs[k % 2].at[pl.ds(row0, cr), pl.ds(0, cc)],
            sem.at[k % 2, slot])

    def round_chunk(k, n):
        _, dst, cr, cc, _, n_chunks = pieces[k]
        if isinstance(n, int):
            slot = n % 2
            if n + 1 < n_chunks:
                chunk_copy(k, n + 1, 1 - slot).start()
        else:
            slot = lax.rem(n, 2)

            @pl.when(n + 1 < n_chunks)
            def _():
                chunk_copy(k, n + 1, 1 - slot).start()
        chunk_copy(k, n, slot).wait()
        r0, c0, row0 = offsets(k, n, slot)
        dst[pl.ds(r0, cr), pl.ds(c0, cc)] = (
            slot_pairs[k % 2][pl.ds(row0, cr), pl.ds(0, cc)].astype(jnp.bfloat16))

    chunk_copy(0, 0, 0).start()
    for k, piece in enumerate(pieces):
        if k + 1 < len(pieces):
            chunk_copy(k + 1, 0, 0).start()
        n_chunks = piece[-1]
        if n_chunks == 1:
            round_chunk(k, 0)
        else:
            lax.fori_loop(0, n_chunks, lambda n, c, k=k: (round_chunk(k, n), c)[1], 0)


def _conv(ext, w_conv, seqs, rows):
    out = None
    for k in range(CONV_W):
        lo = _CONV_HIST - N_CONV_HIST + k
        term = ext[:, lo:lo + rows, :].reshape(seqs * rows, ext.shape[-1]) * w_conv[0, k:k + 1, :]
        out = term if out is None else out + term
    return out


def _pooled(ext, seqs, rows, pos_base, n_groups):
    gw = ext.shape[-1] // n_groups
    pos = (pos_base + lax.broadcasted_iota(jnp.int32, (seqs, rows, V7X_LANES), 1)
           ).astype(jnp.float32)
    out = []
    for gi, w in enumerate(POOL_WINDOWS):
        sl = slice(gi * gw, (gi + 1) * gw)
        s = ext[:, :, sl]
        k = 1
        while k < w:
            s = s + pltpu.roll(s, k, axis=1)
            k *= 2
        inv_cnt = 1.0 / jnp.minimum(jnp.float32(w), pos + 1.0)
        inv_cnt = jnp.concatenate([inv_cnt] * (gw // V7X_LANES), axis=-1)
        cur = slice(_POOL_HIST, _POOL_HIST + rows)
        out.append((s[:, cur, :] * inv_cnt - ext[:, cur, sl]
                    ).reshape(seqs * rows, gw).astype(jnp.bfloat16))
    return out


def _trunk_kernel(*refs, cfg: _Cfg):
    names = ["xp", "pp", "xs", "ps", "sc", "sp"] + list(_SMALL) + list(_BIG)
    names += ["yp", "ys", "ncp", "npp", "ncs", "nps"]
    names += ["up_ext", "vp_ext", "us_ext", "vs_ext", "vs_hist"] + ["vm_" + n for n in _BIG]
    names += ["stage_sem", "sp_sem", "nps_sem"]
    assert len(names) == len(refs)
    r = dict(zip(names, refs))
    up_ext, vp_ext, us_ext, vs_ext = r["up_ext"], r["vp_ext"], r["us_ext"], r["vs_ext"]
    g_mix, w_conv, pool_scale, g_mlp, g_ple, g_final = (r[n] for n in _SMALL)
    vmem = {n: r["vm_" + n] for n in _BIG}
    w_in, w_oc, w_pool, w_o, w_up, w_down, w_pg, w_pp = (vmem[n] for n in _BIG)

    L, Ss, Ls = cfg.prompt_rows, cfg.sample_seqs, cfg.sample_rows
    Rs = Ss * Ls
    D = r["xp"].shape[-1]
    C = w_conv.shape[-1]
    n_groups, GW = w_pool.shape[0], w_pool.shape[1]
    P = n_groups * GW
    HU, HV = _CONV_HIST, _POOL_HIST

    step = pl.program_id(0)
    j = lax.rem(step, cfg.tiles_per_seq)
    seq0 = step * Ss

    @pl.when(step == 0)
    def _():
        _round_weights_to_vmem({n: r[n] for n in _BIG}, vmem, (up_ext.at[0], vp_ext.at[0]),
                               r["stage_sem"])

    vs_hist = r["vs_hist"]
    slot = lax.rem(step, 2)

    def pool_state_in(i, to_slot, first_seq):
        return pltpu.make_async_copy(r["sp"].at[i, pl.ds(first_seq, Ss), :],
                                     vs_hist.at[to_slot, :, HV - N_POOL_HIST + i, :],
                                     r["sp_sem"].at[to_slot, i])

    def pool_state_out(i):
        return pltpu.make_async_copy(vs_ext.at[:, HV + Ls - N_POOL_HIST + i, :],
                                     r["nps"].at[i, pl.ds(seq0, Ss), :], r["nps_sem"].at[i])

    @pl.when(step == 0)
    def _():
        for s in range(2):
            vs_hist[s, :, 0:HV - N_POOL_HIST, :] = jnp.zeros((Ss, HV - N_POOL_HIST, P),
                                                             jnp.float32)
        for i in range(N_POOL_HIST):
            pool_state_in(i, 0, 0).start()

    @pl.when(step > 0)
    def _():
        for i in range(N_POOL_HIST):
            pool_state_out(i).wait()

    for i in range(N_POOL_HIST):
        pool_state_in(i, slot, seq0).wait()
    vs_ext[:, 0:HV, :] = vs_hist[slot]

    @pl.when(step + 1 < pl.num_programs(0))
    def _():
        for i in range(N_POOL_HIST):
            pool_state_in(i, 1 - slot, seq0 + Ss).start()

    @pl.when(j == 0)
    def _():
        up_ext[:, 0:HU, :] = jnp.zeros((1, HU, C), jnp.float32)
        vp_ext[:, 0:HV, :] = jnp.zeros((1, HV, P), jnp.float32)

    @pl.when(j > 0)
    def _():
        up_ext[:, 0:HU, :] = up_ext[:, L:L + HU, :]
        vp_ext[:, 0:HV, :] = vp_ext[:, L:L + HV, :]

    us_ext[:, HU - N_CONV_HIST:HU, :] = r["sc"][...]

    def rows_of(prompt_ref, sample_ref):
        width = prompt_ref.shape[-1]
        return jnp.concatenate([prompt_ref[...].reshape(L, width),
                                sample_ref[...].reshape(Rs, width)], axis=0)

    pe = _dot(rows_of(r["pp"], r["ps"]), w_pp[...])
    xn = _rms(rows_of(r["xp"], r["xs"]), g_mix[...]).astype(jnp.bfloat16)

    o1, o2, o3 = C, 2 * C, 3 * C
    o4 = o3 + P
    o5 = o4 + D
    u = _dot(xn, w_in[:, o1:o2]) * _dot(xn, w_in[:, o2:o3])
    up_ext[:, HU:HU + L, :] = u[0:L].reshape(1, L, C)
    us_ext[:, HU:HU + Ls, :] = u[L:].reshape(Ss, Ls, C)
    v = _dot(xn, w_in[:, o3:o4])
    vp_ext[:, HV:HV + L, :] = v[0:L].reshape(1, L, P)
    vs_ext[:, HV:HV + Ls, :] = v[L:].reshape(Ss, Ls, P)
    b = _dot(xn, w_in[:, 0:o1])
    za = _dot(xn, w_in[:, o4:o5])

    conv = jnp.concatenate([_conv(up_ext, w_conv, 1, L), _conv(us_ext, w_conv, Ss, Ls)], axis=0)
    ya = _dot(b * conv, w_oc[...])

    pooled_p = _pooled(vp_ext, 1, L, j * L, n_groups)
    pooled_s = _pooled(vs_ext, Ss, Ls, PAST_LEN, n_groups)
    gated_a = _sigmoid(za) * ya
    yb = jnp.concatenate(
        [_dot(jnp.concatenate([pooled_p[gi], pooled_s[gi]], axis=0), w_pool[gi])
         for gi in range(n_groups)], axis=-1)
    zb = _dot(xn, w_in[:, o5:])
    h = rows_of(r["xp"], r["xs"]) + _dot(gated_a + _sigmoid(zb) * (yb * pool_scale[...]), w_o[...])

    hn = _rms(h, g_mlp[...]).astype(jnp.bfloat16)
    d_ff = w_up.shape[1]
    acc = h
    mid = _dot(hn, w_up[:, 0:cfg.d_ff_chunk])
    for c0 in range(0, d_ff, cfg.d_ff_chunk):
        nxt = c0 + cfg.d_ff_chunk
        mid_next = _dot(hn, w_up[:, nxt:nxt + cfg.d_ff_chunk]) if nxt < d_ff else None
        act = jnp.square(jnp.maximum(mid, 0.0))
        acc = acc + _dot(act, w_down[c0:nxt, :])
        mid = mid_next
    h = acc

    zg = _dot(_rms(h, g_ple[...]), w_pg[...])
    y = _rms(h + _sigmoid(zg) * pe, g_final[...])
    r["yp"][...] = y[0:L].reshape(1, L, D)
    r["ys"][...] = y[L:].reshape(Ss, Ls, D)

    @pl.when(j == cfg.tiles_per_seq - 1)
    def _():
        r["ncp"][...] = up_ext[:, HU + L - N_CONV_HIST:HU + L, :]
        r["npp"][...] = vp_ext[:, HV + L - N_POOL_HIST:HV + L, :]

    r["ncs"][...] = us_ext[:, HU + Ls - N_CONV_HIST:HU + Ls, :]
    for i in range(N_POOL_HIST):
        pool_state_out(i).start()

    @pl.when(step == pl.num_programs(0) - 1)
    def _():
        for i in range(N_POOL_HIST):
            pool_state_out(i).wait()


def _resident(shape):
    zeros = (0,) * len(shape)
    return pl.BlockSpec(shape, lambda *_: zeros, pipeline_mode=pl.Buffered(1))


def _tile_plan(x_prompt, x_sample):
    n_prompt, seq, _ = x_prompt.shape
    n_sample, dec_seq, _ = x_sample.shape
    prompt_rows = 512
    assert seq % prompt_rows == 0 and prompt_rows >= _POOL_HIST
    assert prompt_rows >= 2 * _STAGE_ROWS
    tiles_per_seq = seq // prompt_rows
    n_steps = n_prompt * tiles_per_seq
    assert n_sample % n_steps == 0 and dec_seq % V7X_SUBLANES == 0
    assert (prompt_rows + (n_sample // n_steps) * dec_seq) % (2 * V7X_SUBLANES) == 0
    return _Cfg(prompt_rows=prompt_rows, tiles_per_seq=tiles_per_seq,
                sample_seqs=n_sample // n_steps, sample_rows=dec_seq, d_ff_chunk=1024), n_steps


def kernel(x_prompt, x_sample, state_conv, state_pool, p_prompt, p_sample, g_mix, w_in, w_conv, w_out_conv, w_pool, pool_scale, w_o, g_mlp, w_up, w_down, g_ple, w_ple_gate, w_ple_proj, g_final):
    depth = w_in.shape[0]
    assert depth == 1, "single-layer trunk only"
    small = (g_mix, w_conv, pool_scale, g_mlp, g_ple, g_final[None, :])
    big = (w_in[0], w_out_conv[0], w_pool[0], w_o[0], w_up[0], w_down[0], w_ple_gate[0],
           w_ple_proj[0])
    cfg, n_steps = _tile_plan(x_prompt, x_sample)
    L, J, Ss, Ls = cfg.prompt_rows, cfg.tiles_per_seq, cfg.sample_seqs, cfg.sample_rows
    n_prompt, _, D = x_prompt.shape
    n_sample = x_sample.shape[0]
    Dp = p_prompt.shape[-1]
    C = w_conv.shape[-1]
    P = w_pool.shape[1] * w_pool.shape[2]
    any_spec = pl.BlockSpec(memory_space=pl.ANY)
    f32 = x_prompt.dtype

    def prompt_tile(i):
        return (i // J, i % J, 0)

    def prompt_seq(i):
        return (i // J, 0, 0)

    def sample_blk(i):
        return (i, 0, 0)

    state_pool_rows = jnp.swapaxes(state_pool[0], 0, 1)
    assert state_pool_rows.shape == (N_POOL_HIST, n_sample, P)

    in_specs = [pl.BlockSpec((1, L, D), prompt_tile), pl.BlockSpec((1, L, Dp), prompt_tile),
                pl.BlockSpec((Ss, Ls, D), sample_blk), pl.BlockSpec((Ss, Ls, Dp), sample_blk),
                pl.BlockSpec((Ss, N_CONV_HIST, C), sample_blk), any_spec]
    in_specs += [_resident(w.shape) for w in small] + [any_spec for _ in big]

    out_shape = [jax.ShapeDtypeStruct(x_prompt.shape, f32),
                 jax.ShapeDtypeStruct(x_sample.shape, f32),
                 jax.ShapeDtypeStruct((n_prompt, N_CONV_HIST, C), f32),
                 jax.ShapeDtypeStruct((n_prompt, N_POOL_HIST, P), f32),
                 jax.ShapeDtypeStruct((n_sample, N_CONV_HIST, C), f32),
                 jax.ShapeDtypeStruct((N_POOL_HIST, n_sample, P), f32)]
    out_specs = [pl.BlockSpec((1, L, D), prompt_tile), pl.BlockSpec((Ss, Ls, D), sample_blk),
                 pl.BlockSpec((1, N_CONV_HIST, C), prompt_seq),
                 pl.BlockSpec((1, N_POOL_HIST, P), prompt_seq),
                 pl.BlockSpec((Ss, N_CONV_HIST, C), sample_blk), any_spec]
    scratch = [pltpu.VMEM((1, _CONV_HIST + L, C), jnp.float32),
               pltpu.VMEM((1, _POOL_HIST + L, P), jnp.float32),
               pltpu.VMEM((Ss, _CONV_HIST + Ls, C), jnp.float32),
               pltpu.VMEM((Ss, _POOL_HIST + Ls, P), jnp.float32),
               pltpu.VMEM((2, Ss, _POOL_HIST, P), jnp.float32)]
    scratch += [pltpu.VMEM(w.shape, jnp.bfloat16) for w in big]
    scratch += [pltpu.SemaphoreType.DMA((2, 2)),
                pltpu.SemaphoreType.DMA((2, N_POOL_HIST)),
                pltpu.SemaphoreType.DMA((N_POOL_HIST,))]

    y_prompt, y_sample, nc_prompt, np_prompt, nc_sample, np_sample_rows = pl.pallas_call(
        functools.partial(_trunk_kernel, cfg=cfg),
        grid=(n_steps,),
        in_specs=in_specs,
        out_specs=out_specs,
        out_shape=out_shape,
        scratch_shapes=scratch,
        compiler_params=pltpu.CompilerParams(
            dimension_semantics=("arbitrary",),
            vmem_limit_bytes=_VMEM_LIMIT),
        name="trunk",
    )(x_prompt, p_prompt[0], x_sample, p_sample[0], state_conv[0], state_pool_rows, *small, *big)

    return (y_prompt, y_sample, nc_prompt[None], np_prompt[None], nc_sample[None],
            jnp.swapaxes(np_sample_rows, 0, 1)[None])
```

```python
import functools
from typing import NamedTuple

import jax
import jax.numpy as jnp
from jax import lax
from jax.experimental import pallas as pl
from jax.experimental.pallas import tpu as pltpu

CONV_W = 3
POOL_WINDOWS = (2, 4, 8, 16)
PAST_LEN = 16384
EPS = 1e-6
N_CONV_HIST = CONV_W - 1
N_POOL_HIST = max(POOL_WINDOWS) - 1

V7X_SUBLANES = 8
V7X_LANES = 128
V7X_VMEM_BYTES = 64 * 1024 * 1024

_CONV_HIST = V7X_SUBLANES
_POOL_HIST = 2 * V7X_SUBLANES
_VMEM_LIMIT = V7X_VMEM_BYTES - 3 * 1024 * 1024

_STAGE_ROWS, _STAGE_COLS = 256, 1024

_SMALL = ("g_mix", "w_conv", "pool_scale", "g_mlp", "g_ple", "g_final")
_BIG = ("w_in", "w_oc", "w_pool", "w_o", "w_up", "w_down", "w_pg", "w_pp")


class _Cfg(NamedTuple):
    prompt_rows: int
    tiles_per_seq: int
    sample_seqs: int
    sample_rows: int
    d_ff_chunk: int


def _rms(x, g):
    ms = jnp.mean(x * x, axis=-1, keepdims=True)
    return x * lax.rsqrt(ms + EPS) * g


def _sigmoid(x):
    return 0.5 * jnp.tanh(0.5 * x) + 0.5


def _dot(a, w):
    return jnp.dot(a.astype(jnp.bfloat16), w, preferred_element_type=jnp.float32)


def _round_weights_to_vmem(hbm, vmem, slots, sem):
    n_slots = len(slots)
    ahead = n_slots - 1
    pieces = []
    for name in _BIG:
        shape = hbm[name].shape
        for g in (range(shape[0]) if len(shape) == 3 else (None,)):
            src = hbm[name] if g is None else hbm[name].at[g]
            dst = vmem[name] if g is None else vmem[name].at[g]
            rows, cols = shape[-2:]
            cr, cc = min(rows, _STAGE_ROWS), min(cols, _STAGE_COLS)
            assert rows % cr == 0 and cols % cc == 0
            pieces.append((src, dst, cr, cc, cols // cc, (rows // cr) * (cols // cc)))

    def looped(k):
        return pieces[k][-1] >= 2 * n_slots and pieces[k][-1] % n_slots == 0

    order = sorted(range(len(pieces)), key=looped)
    base, total = {}, 0
    for k in order:
        base[k], total = total, total + pieces[k][-1]

    def offsets(k, n):
        _, _, cr, cc, per_row, _ = pieces[k]
        if isinstance(n, int):
            return (n // per_row) * cr, (n % per_row) * cc
        return (pl.multiple_of(lax.div(n, per_row) * cr, cr),
                pl.multiple_of(lax.rem(n, per_row) * cc, cc))

    def copy(k, n, slot):
        src, _, cr, cc, _, _ = pieces[k]
        r0, c0 = offsets(k, n)
        return pltpu.make_async_copy(src.at[pl.ds(r0, cr), pl.ds(c0, cc)],
                                     slots[slot].at[0:cr, 0:cc], sem.at[slot])

    def finish(k, n, slot):
        _, dst, cr, cc, _, _ = pieces[k]
        copy(k, n, slot).wait()
        r0, c0 = offsets(k, n)
        dst[pl.ds(r0, cr), pl.ds(c0, cc)] = slots[slot][0:cr, 0:cc].astype(jnp.bfloat16)

    def start_global(g):
        if g < total:
            k = next(k for k in order if base[k] <= g < base[k] + pieces[k][-1])
            copy(k, g - base[k], g % n_slots).start()

    for g in range(ahead):
        start_global(g)
    for pos, k in enumerate(order):
        n_chunks, b = pieces[k][-1], base[k]
        if not looped(k):
            for n in range(n_chunks):
                start_global(b + n + ahead)
                finish(k, n, (b + n) % n_slots)
            continue
        nxt = order[pos + 1] if pos + 1 < len(order) else None
        n_groups = n_chunks // n_slots
        assert nxt is None or pieces[nxt][-1] >= ahead

        def group(it, carry, k=k, b=b, nxt=nxt, n_groups=n_groups):
            for q in range(n_slots):
                n = it * n_slots + q
                pre_slot = (b + q + ahead) % n_slots
                if q + ahead < n_slots:
                    copy(k, n + ahead, pre_slot).start()
                else:
                    m = q + ahead - n_slots

                    @pl.when(it + 1 < n_groups)
                    def _():
                        copy(k, (it + 1) * n_slots + m, pre_slot).start()

                    if nxt is not None:
                        @pl.when(it + 1 == n_groups)
                        def _():
                            copy(nxt, m, pre_slot).start()
                finish(k, n, (b + q) % n_slots)
            return carry

        lax.fori_loop(0, n_groups, group, 0)


def _conv(ext, w_conv, seqs, rows):
    out = None
    for k in range(CONV_W):
        lo = _CONV_HIST - N_CONV_HIST + k
        term = ext[:, lo:lo + rows, :].reshape(seqs * rows, ext.shape[-1]) * w_conv[0, k:k + 1, :]
        out = term if out is None else out + term
    return out


def _pooled(ext, seqs, rows, pos_base, n_groups):
    gw = ext.shape[-1] // n_groups
    pos = (pos_base + lax.broadcasted_iota(jnp.int32, (seqs, rows, V7X_LANES), 1)
           ).astype(jnp.float32)
    out = []
    for gi, w in enumerate(POOL_WINDOWS):
        sl = slice(gi * gw, (gi + 1) * gw)
        s = ext[:, :, sl]
        k = 1
        while k < w:
            s = s + pltpu.roll(s, k, axis=1)
            k *= 2
        inv_cnt = 1.0 / jnp.minimum(jnp.float32(w), pos + 1.0)
        inv_cnt = jnp.concatenate([inv_cnt] * (gw // V7X_LANES), axis=-1)
        cur = slice(_POOL_HIST, _POOL_HIST + rows)
        out.append((s[:, cur, :] * inv_cnt - ext[:, cur, sl]
                    ).reshape(seqs * rows, gw).astype(jnp.bfloat16))
    return out


def _trunk_kernel(*refs, cfg: _Cfg):
    names = ["xp", "pp", "xs", "ps", "sc", "sp"] + list(_SMALL) + list(_BIG)
    names += ["yp", "ys", "ncp", "npp", "ncs", "nps"]
    names += ["up_ext", "vp_ext", "us_ext", "vs_ext", "vs_hist"] + ["vm_" + n for n in _BIG]
    names += ["stage_sem", "sp_sem", "nps_sem"]
    assert len(names) == len(refs)
    r = dict(zip(names, refs))
    up_ext, vp_ext, us_ext, vs_ext = r["up_ext"], r["vp_ext"], r["us_ext"], r["vs_ext"]
    g_mix, w_conv, pool_scale, g_mlp, g_ple, g_final = (r[n] for n in _SMALL)
    vmem = {n: r["vm_" + n] for n in _BIG}
    w_in, w_oc, w_pool, w_o, w_up, w_down, w_pg, w_pp = (vmem[n] for n in _BIG)

    L, Ss, Ls = cfg.prompt_rows, cfg.sample_seqs, cfg.sample_rows
    Rs = Ss * Ls
    D = r["xp"].shape[-1]
    C = w_conv.shape[-1]
    n_groups, GW = w_pool.shape[0], w_pool.shape[1]
    P = n_groups * GW
    HU, HV = _CONV_HIST, _POOL_HIST

    step = pl.program_id(0)
    j = lax.rem(step, cfg.tiles_per_seq)
    seq0 = step * Ss

    @pl.when(step == 0)
    def _():
        slots = [ext.at[0, r0:r0 + _STAGE_ROWS, 0:_STAGE_COLS]
                 for ext in (up_ext, vp_ext) for r0 in (0, _STAGE_ROWS)]
        _round_weights_to_vmem({n: r[n] for n in _BIG}, vmem, slots, r["stage_sem"])

    vs_hist = r["vs_hist"]
    slot = lax.rem(step, 2)

    def pool_state_in(i, to_slot, first_seq):
        return pltpu.make_async_copy(r["sp"].at[i, pl.ds(first_seq, Ss), :],
                                     vs_hist.at[to_slot, :, HV - N_POOL_HIST + i, :],
                                     r["sp_sem"].at[to_slot, i])

    def pool_state_out(i):
        return pltpu.make_async_copy(vs_ext.at[:, HV + Ls - N_POOL_HIST + i, :],
                                     r["nps"].at[i, pl.ds(seq0, Ss), :], r["nps_sem"].at[i])

    @pl.when(step == 0)
    def _():
        for s in range(2):
            vs_hist[s, :, 0:HV - N_POOL_HIST, :] = jnp.zeros((Ss, HV - N_POOL_HIST, P),
                                                             jnp.float32)
        for i in range(N_POOL_HIST):
            pool_state_in(i, 0, 0).start()

    @pl.when(step > 0)
    def _():
        for i in range(N_POOL_HIST):
            pool_state_out(i).wait()

    for i in range(N_POOL_HIST):
        pool_state_in(i, slot, seq0).wait()
    vs_ext[:, 0:HV, :] = vs_hist[slot]

    @pl.when(step + 1 < pl.num_programs(0))
    def _():
        for i in range(N_POOL_HIST):
            pool_state_in(i, 1 - slot, seq0 + Ss).start()

    @pl.when(j == 0)
    def _():
        up_ext[:, 0:HU, :] = jnp.zeros((1, HU, C), jnp.float32)
        vp_ext[:, 0:HV, :] = jnp.zeros((1, HV, P), jnp.float32)

    @pl.when(j > 0)
    def _():
        up_ext[:, 0:HU, :] = up_ext[:, L:L + HU, :]
        vp_ext[:, 0:HV, :] = vp_ext[:, L:L + HV, :]

    us_ext[:, HU - N_CONV_HIST:HU, :] = r["sc"][...]

    def rows_of(prompt_ref, sample_ref):
        width = prompt_ref.shape[-1]
        return jnp.concatenate([prompt_ref[...].reshape(L, width),
                                sample_ref[...].reshape(Rs, width)], axis=0)

    pe = _dot(rows_of(r["pp"], r["ps"]), w_pp[...])
    xn = _rms(rows_of(r["xp"], r["xs"]), g_mix[...]).astype(jnp.bfloat16)

    o1, o2, o3 = C, 2 * C, 3 * C
    o4 = o3 + P
    o5 = o4 + D
    u = _dot(xn, w_in[:, o1:o2]) * _dot(xn, w_in[:, o2:o3])
    up_ext[:, HU:HU + L, :] = u[0:L].reshape(1, L, C)
    us_ext[:, HU:HU + Ls, :] = u[L:].reshape(Ss, Ls, C)
    v = _dot(xn, w_in[:, o3:o4])
    vp_ext[:, HV:HV + L, :] = v[0:L].reshape(1, L, P)
    vs_ext[:, HV:HV + Ls, :] = v[L:].reshape(Ss, Ls, P)
    b = _dot(xn, w_in[:, 0:o1])
    za = _dot(xn, w_in[:, o4:o5])

    conv = jnp.concatenate([_conv(up_ext, w_conv, 1, L), _conv(us_ext, w_conv, Ss, Ls)], axis=0)
    ya = _dot(b * conv, w_oc[...])

    pooled_p = _pooled(vp_ext, 1, L, j * L, n_groups)
    pooled_s = _pooled(vs_ext, Ss, Ls, PAST_LEN, n_groups)
    gated_a = _sigmoid(za) * ya
    yb = jnp.concatenate(
        [_dot(jnp.concatenate([pooled_p[gi], pooled_s[gi]], axis=0), w_pool[gi])
         for gi in range(n_groups)], axis=-1)
    zb = _dot(xn, w_in[:, o5:])
    h = rows_of(r["xp"], r["xs"]) + _dot(gated_a + _sigmoid(zb) * (yb * pool_scale[...]), w_o[...])

    hn = _rms(h, g_mlp[...]).astype(jnp.bfloat16)
    d_ff = w_up.shape[1]
    acc = h
    mid = _dot(hn, w_up[:, 0:cfg.d_ff_chunk])
    for c0 in range(0, d_ff, cfg.d_ff_chunk):
        nxt = c0 + cfg.d_ff_chunk
        mid_next = _dot(hn, w_up[:, nxt:nxt + cfg.d_ff_chunk]) if nxt < d_ff else None
        act = jnp.square(jnp.maximum(mid, 0.0))
        acc = acc + _dot(act, w_down[c0:nxt, :])
        mid = mid_next
    h = acc

    zg = _dot(_rms(h, g_ple[...]), w_pg[...])
    y = _rms(h + _sigmoid(zg) * pe, g_final[...])
    r["yp"][...] = y[0:L].reshape(1, L, D)
    r["ys"][...] = y[L:].reshape(Ss, Ls, D)

    @pl.when(j == cfg.tiles_per_seq - 1)
    def _():
        r["ncp"][...] = up_ext[:, HU + L - N_CONV_HIST:HU + L, :]
        r["npp"][...] = vp_ext[:, HV + L - N_POOL_HIST:HV + L, :]

    r["ncs"][...] = us_ext[:, HU + Ls - N_CONV_HIST:HU + Ls, :]
    for i in range(N_POOL_HIST):
        pool_state_out(i).start()

    @pl.when(step == pl.num_programs(0) - 1)
    def _():
        for i in range(N_POOL_HIST):
            pool_state_out(i).wait()


def _resident(shape):
    zeros = (0,) * len(shape)
    return pl.BlockSpec(shape, lambda *_: zeros, pipeline_mode=pl.Buffered(1))


def _tile_plan(x_prompt, x_sample):
    n_prompt, seq, _ = x_prompt.shape
    n_sample, dec_seq, _ = x_sample.shape
    prompt_rows = 512
    assert seq % prompt_rows == 0 and prompt_rows >= _POOL_HIST
    assert prompt_rows >= 2 * _STAGE_ROWS
    tiles_per_seq = seq // prompt_rows
    n_steps = n_prompt * tiles_per_seq
    assert n_sample % n_steps == 0 and dec_seq % V7X_SUBLANES == 0
    assert (prompt_rows + (n_sample // n_steps) * dec_seq) % (2 * V7X_SUBLANES) == 0
    return _Cfg(prompt_rows=prompt_rows, tiles_per_seq=tiles_per_seq,
                sample_seqs=n_sample // n_steps, sample_rows=dec_seq, d_ff_chunk=1024), n_steps


def kernel(x_prompt, x_sample, state_conv, state_pool, p_prompt, p_sample, g_mix, w_in, w_conv, w_out_conv, w_pool, pool_scale, w_o, g_mlp, w_up, w_down, g_ple, w_ple_gate, w_ple_proj, g_final):
    depth = w_in.shape[0]
    assert depth == 1, "single-layer trunk only"
    small = (g_mix, w_conv, pool_scale, g_mlp, g_ple, g_final[None, :])
    big = (w_in[0], w_out_conv[0], w_pool[0], w_o[0], w_up[0], w_down[0], w_ple_gate[0],
           w_ple_proj[0])
    cfg, n_steps = _tile_plan(x_prompt, x_sample)
    L, J, Ss, Ls = cfg.prompt_rows, cfg.tiles_per_seq, cfg.sample_seqs, cfg.sample_rows
    n_prompt, _, D = x_prompt.shape
    n_sample = x_sample.shape[0]
    Dp = p_prompt.shape[-1]
    C = w_conv.shape[-1]
    P = w_pool.shape[1] * w_pool.shape[2]
    any_spec = pl.BlockSpec(memory_space=pl.ANY)
    f32 = x_prompt.dtype

    def prompt_tile(i):
        return (i // J, i % J, 0)

    def prompt_seq(i):
        return (i // J, 0, 0)

    def sample_blk(i):
        return (i, 0, 0)

    state_pool_rows = jnp.swapaxes(state_pool[0], 0, 1)
    assert state_pool_rows.shape == (N_POOL_HIST, n_sample, P)

    in_specs = [pl.BlockSpec((1, L, D), prompt_tile), pl.BlockSpec((1, L, Dp), prompt_tile),
                pl.BlockSpec((Ss, Ls, D), sample_blk), pl.BlockSpec((Ss, Ls, Dp), sample_blk),
                pl.BlockSpec((Ss, N_CONV_HIST, C), sample_blk), any_spec]
    in_specs += [_resident(w.shape) for w in small] + [any_spec for _ in big]

    out_shape = [jax.ShapeDtypeStruct(x_prompt.shape, f32),
                 jax.ShapeDtypeStruct(x_sample.shape, f32),
                 jax.ShapeDtypeStruct((n_prompt, N_CONV_HIST, C), f32),
                 jax.ShapeDtypeStruct((n_prompt, N_POOL_HIST, P), f32),
                 jax.ShapeDtypeStruct((n_sample, N_CONV_HIST, C), f32),
                 jax.ShapeDtypeStruct((N_POOL_HIST, n_sample, P), f32)]
    out_specs = [pl.BlockSpec((1, L, D), prompt_tile), pl.BlockSpec((Ss, Ls, D), sample_blk),
                 pl.BlockSpec((1, N_CONV_HIST, C), prompt_seq),
                 pl.BlockSpec((1, N_POOL_HIST, P), prompt_seq),
                 pl.BlockSpec((Ss, N_CONV_HIST, C), sample_blk), any_spec]
    scratch = [pltpu.VMEM((1, _CONV_HIST + L, C), jnp.float32),
               pltpu.VMEM((1, _POOL_HIST + L, P), jnp.float32),
               pltpu.VMEM((Ss, _CONV_HIST + Ls, C), jnp.float32),
               pltpu.VMEM((Ss, _POOL_HIST + Ls, P), jnp.float32),
               pltpu.VMEM((2, Ss, _POOL_HIST, P), jnp.float32)]
    scratch += [pltpu.VMEM(w.shape, jnp.bfloat16) for w in big]
    scratch += [pltpu.SemaphoreType.DMA((4,)),
                pltpu.SemaphoreType.DMA((2, N_POOL_HIST)),
                pltpu.SemaphoreType.DMA((N_POOL_HIST,))]

    y_prompt, y_sample, nc_prompt, np_prompt, nc_sample, np_sample_rows = pl.pallas_call(
        functools.partial(_trunk_kernel, cfg=cfg),
        grid=(n_steps,),
        in_specs=in_specs,
        out_specs=out_specs,
        out_shape=out_shape,
        scratch_shapes=scratch,
        compiler_params=pltpu.CompilerParams(
            dimension_semantics=("arbitrary",),
            vmem_limit_bytes=_VMEM_LIMIT),
        name="trunk",
    )(x_prompt, p_prompt[0], x_sample, p_sample[0], state_conv[0], state_pool_rows, *small, *big)

    return (y_prompt, y_sample, nc_prompt[None], np_prompt[None], nc_sample[None],
            jnp.swapaxes(np_sample_rows, 0, 1)[None])
```

```python
import functools
from typing import NamedTuple

import jax
import jax.numpy as jnp
from jax import lax
from jax.experimental import pallas as pl
from jax.experimental.pallas import tpu as pltpu

CONV_W = 3
POOL_WINDOWS = (2, 4, 8, 16)
PAST_LEN = 16384
EPS = 1e-6
N_CONV_HIST = CONV_W - 1
N_POOL_HIST = max(POOL_WINDOWS) - 1

V7X_SUBLANES = 8
V7X_LANES = 128
V7X_VMEM_BYTES = 64 * 1024 * 1024

_CONV_HIST = V7X_SUBLANES
_POOL_HIST = 2 * V7X_SUBLANES
_VMEM_LIMIT = V7X_VMEM_BYTES - 3 * 1024 * 1024

_STAGE_ROWS, _STAGE_COLS = 256, 1024

_SMALL = ("g_mix", "w_conv", "pool_scale", "g_mlp", "g_ple", "g_final")
_BIG = ("w_in", "w_oc", "w_pool", "w_o", "w_up", "w_down", "w_pg", "w_pp")


class _Cfg(NamedTuple):
    n_steps: int
    prompt_rows: int
    tiles_per_seq: int
    sample_seqs: int
    sample_rows: int
    d_ff_chunk: int


def _rms(x, g):
    ms = jnp.mean(x * x, axis=-1, keepdims=True)
    return x * lax.rsqrt(ms + EPS) * g


def _sigmoid(x):
    return 0.5 * jnp.tanh(0.5 * x) + 0.5


def _dot(a, w):
    return jnp.dot(a.astype(jnp.bfloat16), w, preferred_element_type=jnp.float32)


def _round_weights_to_vmem(hbm, vmem, slots, sem):
    n_slots = len(slots)
    ahead = n_slots - 1
    pieces = []
    for name in _BIG:
        shape = hbm[name].shape
        for g in (range(shape[0]) if len(shape) == 3 else (None,)):
            src = hbm[name] if g is None else hbm[name].at[g]
            dst = vmem[name] if g is None else vmem[name].at[g]
            rows, cols = shape[-2:]
            cr, cc = min(rows, _STAGE_ROWS), min(cols, _STAGE_COLS)
            assert rows % cr == 0 and cols % cc == 0
            pieces.append((src, dst, cr, cc, cols // cc, (rows // cr) * (cols // cc)))

    def looped(k):
        return pieces[k][-1] >= 2 * n_slots and pieces[k][-1] % n_slots == 0

    order = sorted(range(len(pieces)), key=looped)
    base, total = {}, 0
    for k in order:
        base[k], total = total, total + pieces[k][-1]

    def offsets(k, n):
        _, _, cr, cc, per_row, _ = pieces[k]
        if isinstance(n, int):
            return (n // per_row) * cr, (n % per_row) * cc
        return (pl.multiple_of(lax.div(n, per_row) * cr, cr),
                pl.multiple_of(lax.rem(n, per_row) * cc, cc))

    def copy(k, n, slot):
        src, _, cr, cc, _, _ = pieces[k]
        r0, c0 = offsets(k, n)
        return pltpu.make_async_copy(src.at[pl.ds(r0, cr), pl.ds(c0, cc)],
                                     slots[slot].at[0:cr, 0:cc], sem.at[slot])

    def finish(k, n, slot):
        _, dst, cr, cc, _, _ = pieces[k]
        copy(k, n, slot).wait()
        r0, c0 = offsets(k, n)
        dst[pl.ds(r0, cr), pl.ds(c0, cc)] = slots[slot][0:cr, 0:cc].astype(jnp.bfloat16)

    def start_global(g):
        if g < total:
            k = next(k for k in order if base[k] <= g < base[k] + pieces[k][-1])
            copy(k, g - base[k], g % n_slots).start()

    for g in range(ahead):
        start_global(g)
    for pos, k in enumerate(order):
        n_chunks, b = pieces[k][-1], base[k]
        if not looped(k):
            for n in range(n_chunks):
                start_global(b + n + ahead)
                finish(k, n, (b + n) % n_slots)
            continue
        nxt = order[pos + 1] if pos + 1 < len(order) else None
        n_groups = n_chunks // n_slots
        assert nxt is None or pieces[nxt][-1] >= ahead

        def group(it, carry, k=k, b=b, nxt=nxt, n_groups=n_groups):
            for q in range(n_slots):
                n = it * n_slots + q
                pre_slot = (b + q + ahead) % n_slots
                if q + ahead < n_slots:
                    copy(k, n + ahead, pre_slot).start()
                else:
                    m = q + ahead - n_slots

                    @pl.when(it + 1 < n_groups)
                    def _():
                        copy(k, (it + 1) * n_slots + m, pre_slot).start()

                    if nxt is not None:
                        @pl.when(it + 1 == n_groups)
                        def _():
                            copy(nxt, m, pre_slot).start()
                finish(k, n, (b + q) % n_slots)
            return carry

        lax.fori_loop(0, n_groups, group, 0)


def _conv(ext, w_conv, seqs, rows):
    out = None
    for k in range(CONV_W):
        lo = _CONV_HIST - N_CONV_HIST + k
        term = ext[:, lo:lo + rows, :].reshape(seqs * rows, ext.shape[-1]) * w_conv[0, k:k + 1, :]
        out = term if out is None else out + term
    return out


def _pooled(ext, seqs, rows, pos_base, n_groups):
    gw = ext.shape[-1] // n_groups
    pos = (pos_base + lax.broadcasted_iota(jnp.int32, (seqs, rows, V7X_LANES), 1)
           ).astype(jnp.float32)
    out = []
    for gi, w in enumerate(POOL_WINDOWS):
        sl = slice(gi * gw, (gi + 1) * gw)
        s = ext[:, :, sl]
        k = 1
        while k < w:
            s = s + pltpu.roll(s, k, axis=1)
            k *= 2
        inv_cnt = 1.0 / jnp.minimum(jnp.float32(w), pos + 1.0)
        inv_cnt = jnp.concatenate([inv_cnt] * (gw // V7X_LANES), axis=-1)
        cur = slice(_POOL_HIST, _POOL_HIST + rows)
        out.append((s[:, cur, :] * inv_cnt - ext[:, cur, sl]
                    ).reshape(seqs * rows, gw).astype(jnp.bfloat16))
    return out


def _trunk_kernel(*refs, cfg: _Cfg):
    names = ["xp", "pp", "xs", "ps", "sc", "sp"] + list(_SMALL) + list(_BIG)
    names += ["yp", "ys", "ncp", "npp", "ncs", "nps"]
    names += ["up_ext", "vp_ext", "us_ext", "vs_ext", "vs_hist", "vs_out"]
    names += ["vm_" + n for n in _BIG]
    names += ["stage_sem", "sp_sem", "nps_sem"]
    assert len(names) == len(refs)
    r = dict(zip(names, refs))
    up_ext, vp_ext, us_ext, vs_ext = r["up_ext"], r["vp_ext"], r["us_ext"], r["vs_ext"]
    g_mix, w_conv, pool_scale, g_mlp, g_ple, g_final = (r[n] for n in _SMALL)
    vmem = {n: r["vm_" + n] for n in _BIG}
    w_in, w_oc, w_pool, w_o, w_up, w_down, w_pg, w_pp = (vmem[n] for n in _BIG)

    L, Ss, Ls = cfg.prompt_rows, cfg.sample_seqs, cfg.sample_rows
    Rs = Ss * Ls
    D = r["xp"].shape[-1]
    C = w_conv.shape[-1]
    n_groups, GW = w_pool.shape[0], w_pool.shape[1]
    P = n_groups * GW
    HU, HV = _CONV_HIST, _POOL_HIST

    step = pl.program_id(0)
    j = lax.rem(step, cfg.tiles_per_seq)
    seq0 = step * Ss

    vs_hist, vs_out = r["vs_hist"], r["vs_out"]
    n_steps = cfg.n_steps
    slot = lax.rem(step, 2)

    def pool_state_in(i, to_slot, first_seq):
        return pltpu.make_async_copy(r["sp"].at[i, pl.ds(first_seq, Ss), :],
                                     vs_hist.at[to_slot, :, HV - N_POOL_HIST + i, :],
                                     r["sp_sem"].at[to_slot, i])

    def pool_state_out(i, from_slot):
        return pltpu.make_async_copy(vs_out.at[from_slot, :, i, :],
                                     r["nps"].at[i, pl.ds(seq0, Ss), :],
                                     r["nps_sem"].at[from_slot, i])

    @pl.when(step == 0)
    def _():
        for s in range(2):
            vs_hist[s, :, 0:HV - N_POOL_HIST, :] = jnp.zeros((Ss, HV - N_POOL_HIST, P),
                                                             jnp.float32)
        for i in range(N_POOL_HIST):
            pool_state_in(i, 0, 0).start()
        slots = [ext.at[0, r0:r0 + _STAGE_ROWS, 0:_STAGE_COLS]
                 for ext in (up_ext, vp_ext) for r0 in (0, _STAGE_ROWS)]
        _round_weights_to_vmem({n: r[n] for n in _BIG}, vmem, slots, r["stage_sem"])

    for i in range(N_POOL_HIST):
        pool_state_in(i, slot, seq0).wait()
    vs_ext[:, 0:HV, :] = vs_hist[slot]

    @pl.when(step + 1 < n_steps)
    def _():
        for i in range(N_POOL_HIST):
            pool_state_in(i, 1 - slot, seq0 + Ss).start()

    @pl.when(j == 0)
    def _():
        up_ext[:, 0:HU, :] = jnp.zeros((1, HU, C), jnp.float32)
        vp_ext[:, 0:HV, :] = jnp.zeros((1, HV, P), jnp.float32)

    @pl.when(j > 0)
    def _():
        up_ext[:, 0:HU, :] = up_ext[:, L:L + HU, :]
        vp_ext[:, 0:HV, :] = vp_ext[:, L:L + HV, :]

    us_ext[:, HU - N_CONV_HIST:HU, :] = r["sc"][...]

    def rows_of(prompt_ref, sample_ref):
        width = prompt_ref.shape[-1]
        return jnp.concatenate([prompt_ref[...].reshape(L, width),
                                sample_ref[...].reshape(Rs, width)], axis=0)

    pe = _dot(rows_of(r["pp"], r["ps"]), w_pp[...])
    xn = _rms(rows_of(r["xp"], r["xs"]), g_mix[...]).astype(jnp.bfloat16)

    o1, o2, o3 = C, 2 * C, 3 * C
    o4 = o3 + P
    o5 = o4 + D
    u = _dot(xn, w_in[:, o1:o2]) * _dot(xn, w_in[:, o2:o3])
    up_ext[:, HU:HU + L, :] = u[0:L].reshape(1, L, C)
    us_ext[:, HU:HU + Ls, :] = u[L:].reshape(Ss, Ls, C)
    v = _dot(xn, w_in[:, o3:o4])
    vp_ext[:, HV:HV + L, :] = v[0:L].reshape(1, L, P)
    vs_ext[:, HV:HV + Ls, :] = v[L:].reshape(Ss, Ls, P)
    b = _dot(xn, w_in[:, 0:o1])
    za = _dot(xn, w_in[:, o4:o5])

    conv = jnp.concatenate([_conv(up_ext, w_conv, 1, L), _conv(us_ext, w_conv, Ss, Ls)], axis=0)
    ya = _dot(b * conv, w_oc[...])

    pooled_p = _pooled(vp_ext, 1, L, j * L, n_groups)
    pooled_s = _pooled(vs_ext, Ss, Ls, PAST_LEN, n_groups)
    gated_a = _sigmoid(za) * ya
    yb = jnp.concatenate(
        [_dot(jnp.concatenate([pooled_p[gi], pooled_s[gi]], axis=0), w_pool[gi])
         for gi in range(n_groups)], axis=-1)
    zb = _dot(xn, w_in[:, o5:])
    h = rows_of(r["xp"], r["xs"]) + _dot(gated_a + _sigmoid(zb) * (yb * pool_scale[...]), w_o[...])

    hn = _rms(h, g_mlp[...]).astype(jnp.bfloat16)
    d_ff = w_up.shape[1]
    acc = h
    mid = _dot(hn, w_up[:, 0:cfg.d_ff_chunk])
    for c0 in range(0, d_ff, cfg.d_ff_chunk):
        nxt = c0 + cfg.d_ff_chunk
        mid_next = _dot(hn, w_up[:, nxt:nxt + cfg.d_ff_chunk]) if nxt < d_ff else None
        act = jnp.square(jnp.maximum(mid, 0.0))
        acc = acc + _dot(act, w_down[c0:nxt, :])
        mid = mid_next
    h = acc

    zg = _dot(_rms(h, g_ple[...]), w_pg[...])
    y = _rms(h + _sigmoid(zg) * pe, g_final[...])
    r["yp"][...] = y[0:L].reshape(1, L, D)
    r["ys"][...] = y[L:].reshape(Ss, Ls, D)

    @pl.when(j == cfg.tiles_per_seq - 1)
    def _():
        r["ncp"][...] = up_ext[:, HU + L - N_CONV_HIST:HU + L, :]
        r["npp"][...] = vp_ext[:, HV + L - N_POOL_HIST:HV + L, :]

    r["ncs"][...] = us_ext[:, HU + Ls - N_CONV_HIST:HU + Ls, :]
    @pl.when(step >= 2)
    def _():
        for i in range(N_POOL_HIST):
            pool_state_out(i, slot).wait()

    vs_out[slot, :, 0:N_POOL_HIST, :] = vs_ext[:, HV + Ls - N_POOL_HIST:HV + Ls, :]
    for i in range(N_POOL_HIST):
        pool_state_out(i, slot).start()

    @pl.when(step == n_steps - 1)
    def _():
        for from_slot in ((1 - slot, slot) if n_steps >= 2 else (slot,)):
            for i in range(N_POOL_HIST):
                pool_state_out(i, from_slot).wait()


def _resident(shape):
    zeros = (0,) * len(shape)
    return pl.BlockSpec(shape, lambda *_: zeros, pipeline_mode=pl.Buffered(1))


def _tile_plan(x_prompt, x_sample):
    n_prompt, seq, _ = x_prompt.shape
    n_sample, dec_seq, _ = x_sample.shape
    prompt_rows = 512
    assert seq % prompt_rows == 0 and prompt_rows >= _POOL_HIST
    assert prompt_rows >= 2 * _STAGE_ROWS
    tiles_per_seq = seq // prompt_rows
    n_steps = n_prompt * tiles_per_seq
    assert n_sample % n_steps == 0 and dec_seq % V7X_SUBLANES == 0
    assert (prompt_rows + (n_sample // n_steps) * dec_seq) % (2 * V7X_SUBLANES) == 0
    return _Cfg(n_steps=n_steps, prompt_rows=prompt_rows, tiles_per_seq=tiles_per_seq,
                sample_seqs=n_sample // n_steps, sample_rows=dec_seq, d_ff_chunk=1024)


def kernel(x_prompt, x_sample, state_conv, state_pool, p_prompt, p_sample, g_mix, w_in, w_conv, w_out_conv, w_pool, pool_scale, w_o, g_mlp, w_up, w_down, g_ple, w_ple_gate, w_ple_proj, g_final):
    depth = w_in.shape[0]
    assert depth == 1, "single-layer trunk only"
    small = (g_mix, w_conv, pool_scale, g_mlp, g_ple, g_final[None, :])
    big = (w_in[0], w_out_conv[0], w_pool[0], w_o[0], w_up[0], w_down[0], w_ple_gate[0],
           w_ple_proj[0])
    cfg = _tile_plan(x_prompt, x_sample)
    L, J, Ss, Ls = cfg.prompt_rows, cfg.tiles_per_seq, cfg.sample_seqs, cfg.sample_rows
    n_prompt, _, D = x_prompt.shape
    n_sample = x_sample.shape[0]
    Dp = p_prompt.shape[-1]
    C = w_conv.shape[-1]
    P = w_pool.shape[1] * w_pool.shape[2]
    any_spec = pl.BlockSpec(memory_space=pl.ANY)
    f32 = x_prompt.dtype

    def prompt_tile(i):
        return (i // J, i % J, 0)

    def prompt_seq(i):
        return (i // J, 0, 0)

    def sample_blk(i):
        return (i, 0, 0)

    state_pool_rows = jnp.swapaxes(state_pool[0], 0, 1)
    assert state_pool_rows.shape == (N_POOL_HIST, n_sample, P)

    in_specs = [pl.BlockSpec((1, L, D), prompt_tile), pl.BlockSpec((1, L, Dp), prompt_tile),
                pl.BlockSpec((Ss, Ls, D), sample_blk), pl.BlockSpec((Ss, Ls, Dp), sample_blk),
                pl.BlockSpec((Ss, N_CONV_HIST, C), sample_blk), any_spec]
    in_specs += [_resident(w.shape) for w in small] + [any_spec for _ in big]

    out_shape = [jax.ShapeDtypeStruct(x_prompt.shape, f32),
                 jax.ShapeDtypeStruct(x_sample.shape, f32),
                 jax.ShapeDtypeStruct((n_prompt, N_CONV_HIST, C), f32),
                 jax.ShapeDtypeStruct((n_prompt, N_POOL_HIST, P), f32),
                 jax.ShapeDtypeStruct((n_sample, N_CONV_HIST, C), f32),
                 jax.ShapeDtypeStruct((N_POOL_HIST, n_sample, P), f32)]
    out_specs = [pl.BlockSpec((1, L, D), prompt_tile), pl.BlockSpec((Ss, Ls, D), sample_blk),
                 pl.BlockSpec((1, N_CONV_HIST, C), prompt_seq),
                 pl.BlockSpec((1, N_POOL_HIST, P), prompt_seq),
                 pl.BlockSpec((Ss, N_CONV_HIST, C), sample_blk), any_spec]
    scratch = [pltpu.VMEM((1, _CONV_HIST + L, C), jnp.float32),
               pltpu.VMEM((1, _POOL_HIST + L, P), jnp.float32),
               pltpu.VMEM((Ss, _CONV_HIST + Ls, C), jnp.float32),
               pltpu.VMEM((Ss, _POOL_HIST + Ls, P), jnp.float32),
               pltpu.VMEM((2, Ss, _POOL_HIST, P), jnp.float32),
               pltpu.VMEM((2, Ss, _POOL_HIST, P), jnp.float32)]
    scratch += [pltpu.VMEM(w.shape, jnp.bfloat16) for w in big]
    scratch += [pltpu.SemaphoreType.DMA((4,)),
                pltpu.SemaphoreType.DMA((2, N_POOL_HIST)),
                pltpu.SemaphoreType.DMA((2, N_POOL_HIST))]

    y_prompt, y_sample, nc_prompt, np_prompt, nc_sample, np_sample_rows = pl.pallas_call(
        functools.partial(_trunk_kernel, cfg=cfg),
        grid=(cfg.n_steps,),
        in_specs=in_specs,
        out_specs=out_specs,
        out_shape=out_shape,
        scratch_shapes=scratch,
        compiler_params=pltpu.CompilerParams(
            dimension_semantics=("arbitrary",),
            vmem_limit_bytes=_VMEM_LIMIT),
        name="trunk",
    )(x_prompt, p_prompt[0], x_sample, p_sample[0], state_conv[0], state_pool_rows, *small, *big)

    return (y_prompt, y_sample, nc_prompt[None], np_prompt[None], nc_sample[None],
            jnp.swapaxes(np_sample_rows, 0, 1)[None])
```

```python
import functools
from typing import NamedTuple

import jax
import jax.numpy as jnp
from jax import lax
from jax.experimental import pallas as pl
from jax.experimental.pallas import tpu as pltpu

CONV_W = 3
POOL_WINDOWS = (2, 4, 8, 16)
PAST_LEN = 16384
EPS = 1e-6
N_CONV_HIST = CONV_W - 1
N_POOL_HIST = max(POOL_WINDOWS) - 1

V7X_SUBLANES = 8
V7X_LANES = 128
V7X_VMEM_BYTES = 64 * 1024 * 1024

_CONV_HIST = V7X_SUBLANES
_POOL_HIST = 2 * V7X_SUBLANES
_VMEM_LIMIT = V7X_VMEM_BYTES - 3 * 1024 * 1024

_STAGE_ROWS, _STAGE_COLS = 256, 1024

_SMALL = ("g_mix", "w_conv", "pool_scale", "g_mlp", "g_ple", "g_final")
_BIG = ("w_in", "w_oc", "w_pool", "w_o", "w_up", "w_down", "w_pg", "w_pp")


class _Cfg(NamedTuple):
    n_steps: int
    prompt_rows: int
    tiles_per_seq: int
    sample_seqs: int
    sample_rows: int
    d_ff_chunk: int


def _rms(x, g):
    ms = jnp.mean(x * x, axis=-1, keepdims=True)
    return x * lax.rsqrt(ms + EPS) * g


def _sigmoid(x):
    return 0.5 * jnp.tanh(0.5 * x) + 0.5


def _dot(a, w):
    return jnp.dot(a.astype(jnp.bfloat16), w, preferred_element_type=jnp.float32)


def _round_weights_to_vmem(hbm, vmem, slots, sem):
    n_slots = len(slots)
    ahead = n_slots - 1
    pieces = []
    for name in _BIG:
        shape = hbm[name].shape
        for g in (range(shape[0]) if len(shape) == 3 else (None,)):
            src = hbm[name] if g is None else hbm[name].at[g]
            dst = vmem[name] if g is None else vmem[name].at[g]
            rows, cols = shape[-2:]
            cr, cc = min(rows, _STAGE_ROWS), min(cols, _STAGE_COLS)
            assert rows % cr == 0 and cols % cc == 0
            pieces.append((src, dst, cr, cc, cols // cc, (rows // cr) * (cols // cc)))

    def looped(k):
        return pieces[k][-1] >= 2 * n_slots and pieces[k][-1] % n_slots == 0

    order = sorted(range(len(pieces)), key=looped)
    base, total = {}, 0
    for k in order:
        base[k], total = total, total + pieces[k][-1]

    def offsets(k, n):
        _, _, cr, cc, per_row, _ = pieces[k]
        if isinstance(n, int):
            return (n // per_row) * cr, (n % per_row) * cc
        return (pl.multiple_of(lax.div(n, per_row) * cr, cr),
                pl.multiple_of(lax.rem(n, per_row) * cc, cc))

    def copy(k, n, slot):
        src, _, cr, cc, _, _ = pieces[k]
        r0, c0 = offsets(k, n)
        return pltpu.make_async_copy(src.at[pl.ds(r0, cr), pl.ds(c0, cc)],
                                     slots[slot].at[0:cr, 0:cc], sem.at[slot])

    def finish(k, n, slot):
        _, dst, cr, cc, _, _ = pieces[k]
        copy(k, n, slot).wait()
        r0, c0 = offsets(k, n)
        dst[pl.ds(r0, cr), pl.ds(c0, cc)] = slots[slot][0:cr, 0:cc].astype(jnp.bfloat16)

    def start_global(g):
        if g < total:
            k = next(k for k in order if base[k] <= g < base[k] + pieces[k][-1])
            copy(k, g - base[k], g % n_slots).start()

    for g in range(ahead):
        start_global(g)
    for pos, k in enumerate(order):
        n_chunks, b = pieces[k][-1], base[k]
        if not looped(k):
            for n in range(n_chunks):
                start_global(b + n + ahead)
                finish(k, n, (b + n) % n_slots)
            continue
        nxt = order[pos + 1] if pos + 1 < len(order) else None
        n_groups = n_chunks // n_slots
        assert nxt is None or pieces[nxt][-1] >= ahead

        def group(it, carry, k=k, b=b, nxt=nxt, n_groups=n_groups):
            for q in range(n_slots):
                n = it * n_slots + q
                pre_slot = (b + q + ahead) % n_slots
                if q + ahead < n_slots:
                    copy(k, n + ahead, pre_slot).start()
                else:
                    m = q + ahead - n_slots

                    @pl.when(it + 1 < n_groups)
                    def _():
                        copy(k, (it + 1) * n_slots + m, pre_slot).start()

                    if nxt is not None:
                        @pl.when(it + 1 == n_groups)
                        def _():
                            copy(nxt, m, pre_slot).start()
                finish(k, n, (b + q) % n_slots)
            return carry

        lax.fori_loop(0, n_groups, group, 0)


def _conv(ext, w_conv, seqs, rows):
    out = None
    for k in range(CONV_W):
        lo = _CONV_HIST - N_CONV_HIST + k
        term = ext[:, lo:lo + rows, :].reshape(seqs * rows, ext.shape[-1]) * w_conv[0, k:k + 1, :]
        out = term if out is None else out + term
    return out


def _pooled(ext, seqs, rows, pos_base, n_groups):
    gw = ext.shape[-1] // n_groups
    pos = (pos_base + lax.broadcasted_iota(jnp.int32, (seqs, rows, V7X_LANES), 1)
           ).astype(jnp.float32)
    out = []
    for gi, w in enumerate(POOL_WINDOWS):
        sl = slice(gi * gw, (gi + 1) * gw)
        s = ext[:, :, sl]
        k = 1
        while k < w:
            s = s + pltpu.roll(s, k, axis=1)
            k *= 2
        inv_cnt = 1.0 / jnp.minimum(jnp.float32(w), pos + 1.0)
        inv_cnt = jnp.concatenate([inv_cnt] * (gw // V7X_LANES), axis=-1)
        cur = slice(_POOL_HIST, _POOL_HIST + rows)
        out.append((s[:, cur, :] * inv_cnt - ext[:, cur, sl]
                    ).reshape(seqs * rows, gw).astype(jnp.bfloat16))
    return out


def _trunk_kernel(*refs, cfg: _Cfg):
    names = ["xp", "pp", "xs", "ps", "sc", "sp"] + list(_SMALL) + list(_BIG)
    names += ["yp", "ys", "ncp", "npp", "ncs", "nps"]
    names += ["up_ext", "vp_ext", "us_ext", "vs_ext", "vs_hist", "vs_out", "npp_buf"]
    names += ["vm_" + n for n in _BIG]
    names += ["stage_sem", "sp_sem", "nps_sem", "keep_sem", "npp_sem"]
    assert len(names) == len(refs)
    r = dict(zip(names, refs))
    up_ext, vp_ext, us_ext, vs_ext = r["up_ext"], r["vp_ext"], r["us_ext"], r["vs_ext"]
    g_mix, w_conv, pool_scale, g_mlp, g_ple, g_final = (r[n] for n in _SMALL)
    vmem = {n: r["vm_" + n] for n in _BIG}
    w_in, w_oc, w_pool, w_o, w_up, w_down, w_pg, w_pp = (vmem[n] for n in _BIG)

    L, Ss, Ls = cfg.prompt_rows, cfg.sample_seqs, cfg.sample_rows
    Rs = Ss * Ls
    D = r["xp"].shape[-1]
    C = w_conv.shape[-1]
    n_groups, GW = w_pool.shape[0], w_pool.shape[1]
    P = n_groups * GW
    HU, HV = _CONV_HIST, _POOL_HIST

    step = pl.program_id(0)
    j = lax.rem(step, cfg.tiles_per_seq)
    seq0 = step * Ss

    vs_hist, vs_out = r["vs_hist"], r["vs_out"]
    n_steps = cfg.n_steps
    slot = lax.rem(step, 2)

    def pool_state_in(i, to_slot, first_seq):
        return pltpu.make_async_copy(r["sp"].at[i, pl.ds(first_seq, Ss), :],
                                     vs_hist.at[to_slot, :, HV - N_POOL_HIST + i, :],
                                     r["sp_sem"].at[to_slot, i])

    n_kept = max(0, N_POOL_HIST - Ls)
    n_new = N_POOL_HIST - n_kept

    def pool_state_keep():
        return pltpu.make_async_copy(r["sp"].at[pl.ds(N_POOL_HIST - n_kept, n_kept)],
                                     r["nps"].at[pl.ds(0, n_kept)], r["keep_sem"].at[0])

    def pool_state_out(k, from_slot):
        return pltpu.make_async_copy(vs_out.at[from_slot, :, k, :],
                                     r["nps"].at[n_kept + k, pl.ds(seq0, Ss), :],
                                     r["nps_sem"].at[from_slot, k])

    @pl.when(step == 0)
    def _():
        for s in range(2):
            vs_hist[s, :, 0:HV - N_POOL_HIST, :] = jnp.zeros((Ss, HV - N_POOL_HIST, P),
                                                             jnp.float32)
        for i in range(N_POOL_HIST):
            pool_state_in(i, 0, 0).start()
        if n_kept:
            pool_state_keep().start()
        slots = [ext.at[0, r0:r0 + _STAGE_ROWS, 0:_STAGE_COLS]
                 for ext in (up_ext, vp_ext) for r0 in (0, _STAGE_ROWS)]
        _round_weights_to_vmem({n: r[n] for n in _BIG}, vmem, slots, r["stage_sem"])

    for i in range(N_POOL_HIST):
        pool_state_in(i, slot, seq0).wait()
    vs_ext[:, 0:HV, :] = vs_hist[slot]

    @pl.when(step + 1 < n_steps)
    def _():
        for i in range(N_POOL_HIST):
            pool_state_in(i, 1 - slot, seq0 + Ss).start()

    @pl.when(j == 0)
    def _():
        up_ext[:, 0:HU, :] = jnp.zeros((1, HU, C), jnp.float32)
        vp_ext[:, 0:HV, :] = jnp.zeros((1, HV, P), jnp.float32)

    @pl.when(j > 0)
    def _():
        up_ext[:, 0:HU, :] = up_ext[:, L:L + HU, :]
        vp_ext[:, 0:HV, :] = vp_ext[:, L:L + HV, :]

    us_ext[:, HU - N_CONV_HIST:HU, :] = r["sc"][...]

    def rows_of(prompt_ref, sample_ref):
        width = prompt_ref.shape[-1]
        return jnp.concatenate([prompt_ref[...].reshape(L, width),
                                sample_ref[...].reshape(Rs, width)], axis=0)

    pe = _dot(rows_of(r["pp"], r["ps"]), w_pp[...])
    xn = _rms(rows_of(r["xp"], r["xs"]), g_mix[...]).astype(jnp.bfloat16)

    o1, o2, o3 = C, 2 * C, 3 * C
    o4 = o3 + P
    o5 = o4 + D
    u = _dot(xn, w_in[:, o1:o2]) * _dot(xn, w_in[:, o2:o3])
    up_ext[:, HU:HU + L, :] = u[0:L].reshape(1, L, C)
    us_ext[:, HU:HU + Ls, :] = u[L:].reshape(Ss, Ls, C)
    v = _dot(xn, w_in[:, o3:o4])
    vp_ext[:, HV:HV + L, :] = v[0:L].reshape(1, L, P)
    vs_ext[:, HV:HV + Ls, :] = v[L:].reshape(Ss, Ls, P)
    b = _dot(xn, w_in[:, 0:o1])
    za = _dot(xn, w_in[:, o4:o5])

    conv = jnp.concatenate([_conv(up_ext, w_conv, 1, L), _conv(us_ext, w_conv, Ss, Ls)], axis=0)
    ya = _dot(b * conv, w_oc[...])

    pooled_p = _pooled(vp_ext, 1, L, j * L, n_groups)
    pooled_s = _pooled(vs_ext, Ss, Ls, PAST_LEN, n_groups)
    gated_a = _sigmoid(za) * ya
    yb = jnp.concatenate(
        [_dot(jnp.concatenate([pooled_p[gi], pooled_s[gi]], axis=0), w_pool[gi])
         for gi in range(n_groups)], axis=-1)
    zb = _dot(xn, w_in[:, o5:])
    h = rows_of(r["xp"], r["xs"]) + _dot(gated_a + _sigmoid(zb) * (yb * pool_scale[...]), w_o[...])

    hn = _rms(h, g_mlp[...]).astype(jnp.bfloat16)
    d_ff = w_up.shape[1]
    acc = h
    mid = _dot(hn, w_up[:, 0:cfg.d_ff_chunk])
    for c0 in range(0, d_ff, cfg.d_ff_chunk):
        nxt = c0 + cfg.d_ff_chunk
        mid_next = _dot(hn, w_up[:, nxt:nxt + cfg.d_ff_chunk]) if nxt < d_ff else None
        act = jnp.square(jnp.maximum(mid, 0.0))
        acc = acc + _dot(act, w_down[c0:nxt, :])
        mid = mid_next
    h = acc

    zg = _dot(_rms(h, g_ple[...]), w_pg[...])
    y = _rms(h + _sigmoid(zg) * pe, g_final[...])
    r["yp"][...] = y[0:L].reshape(1, L, D)
    r["ys"][...] = y[L:].reshape(Ss, Ls, D)

    @pl.when(j == cfg.tiles_per_seq - 1)
    def _():
        r["ncp"][...] = up_ext[:, HU + L - N_CONV_HIST:HU + L, :]
        seq = lax.div(step, cfg.tiles_per_seq)

        def prompt_state_out(i):
            return pltpu.make_async_copy(r["npp_buf"].at[i], r["npp"].at[i, seq],
                                         r["npp_sem"].at[i])

        @pl.when(seq > 0)
        def _():
            for i in range(N_POOL_HIST):
                prompt_state_out(i).wait()

        r["npp_buf"][0:N_POOL_HIST, :] = vp_ext[0, HV + L - N_POOL_HIST:HV + L, :]
        for i in range(N_POOL_HIST):
            prompt_state_out(i).start()

        @pl.when(step == n_steps - 1)
        def _():
            for i in range(N_POOL_HIST):
                prompt_state_out(i).wait()

    r["ncs"][...] = us_ext[:, HU + Ls - N_CONV_HIST:HU + Ls, :]
    @pl.when(step >= 2)
    def _():
        for k in range(n_new):
            pool_state_out(k, slot).wait()

    vs_out[slot, :, 0:n_new, :] = vs_ext[:, HV + Ls - n_new:HV + Ls, :]
    for k in range(n_new):
        pool_state_out(k, slot).start()

    @pl.when(step == n_steps - 1)
    def _():
        for from_slot in ((1 - slot, slot) if n_steps >= 2 else (slot,)):
            for k in range(n_new):
                pool_state_out(k, from_slot).wait()
        if n_kept:
            pool_state_keep().wait()


def _resident(shape):
    zeros = (0,) * len(shape)
    return pl.BlockSpec(shape, lambda *_: zeros, pipeline_mode=pl.Buffered(1))


def _tile_plan(x_prompt, x_sample):
    n_prompt, seq, _ = x_prompt.shape
    n_sample, dec_seq, _ = x_sample.shape
    prompt_rows = 512
    assert seq % prompt_rows == 0 and prompt_rows >= _POOL_HIST
    assert prompt_rows >= 2 * _STAGE_ROWS
    tiles_per_seq = seq // prompt_rows
    n_steps = n_prompt * tiles_per_seq
    assert n_sample % n_steps == 0 and dec_seq % V7X_SUBLANES == 0
    assert (prompt_rows + (n_sample // n_steps) * dec_seq) % (2 * V7X_SUBLANES) == 0
    return _Cfg(n_steps=n_steps, prompt_rows=prompt_rows, tiles_per_seq=tiles_per_seq,
                sample_seqs=n_sample // n_steps, sample_rows=dec_seq, d_ff_chunk=1024)


def kernel(x_prompt, x_sample, state_conv, state_pool, p_prompt, p_sample, g_mix, w_in, w_conv, w_out_conv, w_pool, pool_scale, w_o, g_mlp, w_up, w_down, g_ple, w_ple_gate, w_ple_proj, g_final):
    depth = w_in.shape[0]
    assert depth == 1, "single-layer trunk only"
    small = (g_mix, w_conv, pool_scale, g_mlp, g_ple, g_final[None, :])
    big = (w_in[0], w_out_conv[0], w_pool[0], w_o[0], w_up[0], w_down[0], w_ple_gate[0],
           w_ple_proj[0])
    cfg = _tile_plan(x_prompt, x_sample)
    L, J, Ss, Ls = cfg.prompt_rows, cfg.tiles_per_seq, cfg.sample_seqs, cfg.sample_rows
    n_prompt, _, D = x_prompt.shape
    n_sample = x_sample.shape[0]
    Dp = p_prompt.shape[-1]
    C = w_conv.shape[-1]
    P = w_pool.shape[1] * w_pool.shape[2]
    any_spec = pl.BlockSpec(memory_space=pl.ANY)
    f32 = x_prompt.dtype

    def prompt_tile(i):
        return (i // J, i % J, 0)

    def prompt_seq(i):
        return (i // J, 0, 0)

    def sample_blk(i):
        return (i, 0, 0)

    state_pool_rows = jnp.swapaxes(state_pool[0], 0, 1)
    assert state_pool_rows.shape == (N_POOL_HIST, n_sample, P)

    in_specs = [pl.BlockSpec((1, L, D), prompt_tile), pl.BlockSpec((1, L, Dp), prompt_tile),
                pl.BlockSpec((Ss, Ls, D), sample_blk), pl.BlockSpec((Ss, Ls, Dp), sample_blk),
                pl.BlockSpec((Ss, N_CONV_HIST, C), sample_blk), any_spec]
    in_specs += [_resident(w.shape) for w in small] + [any_spec for _ in big]

    out_shape = [jax.ShapeDtypeStruct(x_prompt.shape, f32),
                 jax.ShapeDtypeStruct(x_sample.shape, f32),
                 jax.ShapeDtypeStruct((n_prompt, N_CONV_HIST, C), f32),
                 jax.ShapeDtypeStruct((N_POOL_HIST, n_prompt, P), f32),
                 jax.ShapeDtypeStruct((n_sample, N_CONV_HIST, C), f32),
                 jax.ShapeDtypeStruct((N_POOL_HIST, n_sample, P), f32)]
    out_specs = [pl.BlockSpec((1, L, D), prompt_tile), pl.BlockSpec((Ss, Ls, D), sample_blk),
                 pl.BlockSpec((1, N_CONV_HIST, C), prompt_seq),
                 any_spec,
                 pl.BlockSpec((Ss, N_CONV_HIST, C), sample_blk), any_spec]
    scratch = [pltpu.VMEM((1, _CONV_HIST + L, C), jnp.float32),
               pltpu.VMEM((1, _POOL_HIST + L, P), jnp.float32),
               pltpu.VMEM((Ss, _CONV_HIST + Ls, C), jnp.float32),
               pltpu.VMEM((Ss, _POOL_HIST + Ls, P), jnp.float32),
               pltpu.VMEM((2, Ss, _POOL_HIST, P), jnp.float32),
               pltpu.VMEM((2, Ss, _POOL_HIST, P), jnp.float32),
               pltpu.VMEM((_POOL_HIST, P), jnp.float32)]
    scratch += [pltpu.VMEM(w.shape, jnp.bfloat16) for w in big]
    scratch += [pltpu.SemaphoreType.DMA((4,)),
                pltpu.SemaphoreType.DMA((2, N_POOL_HIST)),
                pltpu.SemaphoreType.DMA((2, N_POOL_HIST)),
                pltpu.SemaphoreType.DMA((1,)),
                pltpu.SemaphoreType.DMA((N_POOL_HIST,))]

    y_prompt, y_sample, nc_prompt, np_prompt_rows, nc_sample, np_sample_rows = pl.pallas_call(
        functools.partial(_trunk_kernel, cfg=cfg),
        grid=(cfg.n_steps,),
        in_specs=in_specs,
        out_specs=out_specs,
        out_shape=out_shape,
        scratch_shapes=scratch,
        compiler_params=pltpu.CompilerParams(
            dimension_semantics=("arbitrary",),
            vmem_limit_bytes=_VMEM_LIMIT),
        name="trunk",
    )(x_prompt, p_prompt[0], x_sample, p_sample[0], state_conv[0], state_pool_rows, *small, *big)

    return (y_prompt, y_sample, nc_prompt[None], jnp.swapaxes(np_prompt_rows, 0, 1)[None],
            nc_sample[None], jnp.swapaxes(np_sample_rows, 0, 1)[None])
```

```python
import functools
from typing import NamedTuple

import jax
import jax.numpy as jnp
from jax import lax
from jax.experimental import pallas as pl
from jax.experimental.pallas import tpu as pltpu

CONV_W = 3
POOL_WINDOWS = (2, 4, 8, 16)
PAST_LEN = 16384
EPS = 1e-6
N_CONV_HIST = CONV_W - 1
N_POOL_HIST = max(POOL_WINDOWS) - 1

V7X_SUBLANES = 8
V7X_LANES = 128
V7X_VMEM_BYTES = 64 * 1024 * 1024

_CONV_HIST = V7X_SUBLANES
_POOL_HIST = 2 * V7X_SUBLANES
_VMEM_LIMIT = V7X_VMEM_BYTES - 3 * 1024 * 1024

_STAGE_ROWS, _STAGE_COLS = 256, 1024

_SMALL = ("g_mix", "w_conv", "pool_scale", "g_mlp", "g_ple", "g_final")
_BIG = ("w_in", "w_oc", "w_pool", "w_o", "w_up", "w_down", "w_pg", "w_pp")


class _Cfg(NamedTuple):
    n_steps: int
    prompt_rows: int
    tiles_per_seq: int
    sample_seqs: int
    sample_rows: int
    d_ff_chunk: int


def _rms(x, g):
    ms = jnp.mean(x * x, axis=-1, keepdims=True)
    return x * lax.rsqrt(ms + EPS) * g


def _sigmoid(x):
    return 0.5 * jnp.tanh(0.5 * x) + 0.5


def _dot(a, w):
    return jnp.dot(a.astype(jnp.bfloat16), w, preferred_element_type=jnp.float32)


def _round_weights_to_vmem(hbm, vmem, slots, sem):
    n_slots = len(slots)
    ahead = n_slots - 1
    pieces = []
    for name in _BIG:
        shape = hbm[name].shape
        for g in (range(shape[0]) if len(shape) == 3 else (None,)):
            src = hbm[name] if g is None else hbm[name].at[g]
            dst = vmem[name] if g is None else vmem[name].at[g]
            rows, cols = shape[-2:]
            cr, cc = min(rows, _STAGE_ROWS), min(cols, _STAGE_COLS)
            assert rows % cr == 0 and cols % cc == 0
            pieces.append((src, dst, cr, cc, cols // cc, (rows // cr) * (cols // cc)))

    def looped(k):
        return pieces[k][-1] >= 2 * n_slots and pieces[k][-1] % n_slots == 0

    order = sorted(range(len(pieces)), key=looped)
    base, total = {}, 0
    for k in order:
        base[k], total = total, total + pieces[k][-1]

    def offsets(k, n):
        _, _, cr, cc, per_row, _ = pieces[k]
        if isinstance(n, int):
            return (n // per_row) * cr, (n % per_row) * cc
        return (pl.multiple_of(lax.div(n, per_row) * cr, cr),
                pl.multiple_of(lax.rem(n, per_row) * cc, cc))

    def copy(k, n, slot):
        src, _, cr, cc, _, _ = pieces[k]
        r0, c0 = offsets(k, n)
        return pltpu.make_async_copy(src.at[pl.ds(r0, cr), pl.ds(c0, cc)],
                                     slots[slot].at[0:cr, 0:cc], sem.at[slot])

    def finish(k, n, slot):
        _, dst, cr, cc, _, _ = pieces[k]
        copy(k, n, slot).wait()
        r0, c0 = offsets(k, n)
        dst[pl.ds(r0, cr), pl.ds(c0, cc)] = slots[slot][0:cr, 0:cc].astype(jnp.bfloat16)

    def start_global(g):
        if g < total:
            k = next(k for k in order if base[k] <= g < base[k] + pieces[k][-1])
            copy(k, g - base[k], g % n_slots).start()

    for g in range(ahead):
        start_global(g)
    for pos, k in enumerate(order):
        n_chunks, b = pieces[k][-1], base[k]
        if not looped(k):
            for n in range(n_chunks):
                start_global(b + n + ahead)
                finish(k, n, (b + n) % n_slots)
            continue
        nxt = order[pos + 1] if pos + 1 < len(order) else None
        n_groups = n_chunks // n_slots
        assert nxt is None or pieces[nxt][-1] >= ahead

        def group(it, carry, k=k, b=b, nxt=nxt, n_groups=n_groups):
            for q in range(n_slots):
                n = it * n_slots + q
                pre_slot = (b + q + ahead) % n_slots
                if q + ahead < n_slots:
                    copy(k, n + ahead, pre_slot).start()
                else:
                    m = q + ahead - n_slots

                    @pl.when(it + 1 < n_groups)
                    def _():
                        copy(k, (it + 1) * n_slots + m, pre_slot).start()

                    if nxt is not None:
                        @pl.when(it + 1 == n_groups)
                        def _():
                            copy(nxt, m, pre_slot).start()
                finish(k, n, (b + q) % n_slots)
            return carry

        lax.fori_loop(0, n_groups, group, 0)


def _conv(ext, w_conv, seqs, rows):
    out = None
    for k in range(CONV_W):
        lo = _CONV_HIST - N_CONV_HIST + k
        term = ext[:, lo:lo + rows, :].reshape(seqs * rows, ext.shape[-1]) * w_conv[k]
        out = term if out is None else out + term
    return out


def _pooled(ext, seqs, rows, pos_base, n_groups):
    gw = ext.shape[-1] // n_groups
    pos = (pos_base + lax.broadcasted_iota(jnp.int32, (seqs, rows, V7X_LANES), 1)
           ).astype(jnp.float32)
    out = []
    for gi, w in enumerate(POOL_WINDOWS):
        sl = slice(gi * gw, (gi + 1) * gw)
        s = ext[:, :, sl]
        k = 1
        while k < w:
            s = s + pltpu.roll(s, k, axis=1)
            k *= 2
        inv_cnt = 1.0 / jnp.minimum(jnp.float32(w), pos + 1.0)
        inv_cnt = jnp.concatenate([inv_cnt] * (gw // V7X_LANES), axis=-1)
        cur = slice(_POOL_HIST, _POOL_HIST + rows)
        out.append((s[:, cur, :] * inv_cnt - ext[:, cur, sl]
                    ).reshape(seqs * rows, gw).astype(jnp.bfloat16))
    return out


def _trunk_kernel(*refs, cfg: _Cfg):
    names = ["xp", "pp", "xs", "ps", "sc", "sp"] + list(_SMALL) + list(_BIG)
    names += ["yp", "ys", "ncp", "npp", "ncs", "nps"]
    names += ["up_ext", "vp_ext", "us_ext", "vs_ext", "vs_hist", "vs_out", "npp_buf"]
    names += ["vm_" + n for n in _BIG]
    names += ["stage_sem", "sp_sem", "nps_sem", "keep_sem", "npp_sem"]
    assert len(names) == len(refs)
    r = dict(zip(names, refs))
    up_ext, vp_ext, us_ext, vs_ext = r["up_ext"], r["vp_ext"], r["us_ext"], r["vs_ext"]
    g_mix, w_conv, pool_scale, g_mlp, g_ple, g_final = (r[n] for n in _SMALL)
    vmem = {n: r["vm_" + n] for n in _BIG}
    w_in, w_oc, w_pool, w_o, w_up, w_down, w_pg, w_pp = (vmem[n] for n in _BIG)

    L, Ss, Ls = cfg.prompt_rows, cfg.sample_seqs, cfg.sample_rows
    Rs = Ss * Ls
    D = r["xp"].shape[-1]
    C = w_conv.shape[-1]
    n_groups, GW = w_pool.shape[0], w_pool.shape[1]
    P = n_groups * GW
    HU, HV = _CONV_HIST, _POOL_HIST

    step = pl.program_id(0)
    j = lax.rem(step, cfg.tiles_per_seq)
    seq0 = step * Ss

    vs_hist, vs_out = r["vs_hist"], r["vs_out"]
    n_steps = cfg.n_steps
    slot = lax.rem(step, 2)

    def pool_state_in(i, to_slot, first_seq):
        return pltpu.make_async_copy(r["sp"].at[i, pl.ds(first_seq, Ss), :],
                                     vs_hist.at[to_slot, :, HV - N_POOL_HIST + i, :],
                                     r["sp_sem"].at[to_slot, i])

    n_kept = max(0, N_POOL_HIST - Ls)
    n_new = N_POOL_HIST - n_kept

    def pool_state_keep():
        return pltpu.make_async_copy(r["sp"].at[pl.ds(N_POOL_HIST - n_kept, n_kept)],
                                     r["nps"].at[pl.ds(0, n_kept)], r["keep_sem"].at[0])

    def pool_state_out(k, from_slot):
        return pltpu.make_async_copy(vs_out.at[from_slot, :, k, :],
                                     r["nps"].at[n_kept + k, pl.ds(seq0, Ss), :],
                                     r["nps_sem"].at[from_slot, k])

    @pl.when(step == 0)
    def _():
        for s in range(2):
            vs_hist[s, :, 0:HV - N_POOL_HIST, :] = jnp.zeros((Ss, HV - N_POOL_HIST, P),
                                                             jnp.float32)
        for i in range(N_POOL_HIST):
            pool_state_in(i, 0, 0).start()
        if n_kept:
            pool_state_keep().start()
        slots = [ext.at[0, r0:r0 + _STAGE_ROWS, 0:_STAGE_COLS]
                 for ext in (up_ext, vp_ext) for r0 in (0, _STAGE_ROWS)]
        _round_weights_to_vmem({n: r[n] for n in _BIG}, vmem, slots, r["stage_sem"])

    for i in range(N_POOL_HIST):
        pool_state_in(i, slot, seq0).wait()
    vs_ext[:, 0:HV, :] = vs_hist[slot]

    @pl.when(step + 1 < n_steps)
    def _():
        for i in range(N_POOL_HIST):
            pool_state_in(i, 1 - slot, seq0 + Ss).start()

    @pl.when(j == 0)
    def _():
        up_ext[:, 0:HU, :] = jnp.zeros((1, HU, C), jnp.float32)
        vp_ext[:, 0:HV, :] = jnp.zeros((1, HV, P), jnp.float32)

    @pl.when(j > 0)
    def _():
        up_ext[:, 0:HU, :] = up_ext[:, L:L + HU, :]
        vp_ext[:, 0:HV, :] = vp_ext[:, L:L + HV, :]

    us_ext[:, HU - N_CONV_HIST:HU, :] = r["sc"][...]

    def rows_of(prompt_ref, sample_ref):
        width = prompt_ref.shape[-1]
        return jnp.concatenate([prompt_ref[...].reshape(L, width),
                                sample_ref[...].reshape(Rs, width)], axis=0)

    pe = _dot(rows_of(r["pp"], r["ps"]), w_pp[...])
    xn = _rms(rows_of(r["xp"], r["xs"]), g_mix[...]).astype(jnp.bfloat16)

    o1, o2, o3 = C, 2 * C, 3 * C
    o4 = o3 + P
    o5 = o4 + D
    u = _dot(xn, w_in[:, o1:o2]) * _dot(xn, w_in[:, o2:o3])
    up_ext[:, HU:HU + L, :] = u[0:L].reshape(1, L, C)
    us_ext[:, HU:HU + Ls, :] = u[L:].reshape(Ss, Ls, C)
    v = _dot(xn, w_in[:, o3:o4])
    vp_ext[:, HV:HV + L, :] = v[0:L].reshape(1, L, P)
    vs_ext[:, HV:HV + Ls, :] = v[L:].reshape(Ss, Ls, P)
    b = _dot(xn, w_in[:, 0:o1])
    za = _dot(xn, w_in[:, o4:o5])

    conv = jnp.concatenate([_conv(up_ext, w_conv, 1, L), _conv(us_ext, w_conv, Ss, Ls)], axis=0)
    ya = _dot(b * conv, w_oc[...])

    pooled_p = _pooled(vp_ext, 1, L, j * L, n_groups)
    pooled_s = _pooled(vs_ext, Ss, Ls, PAST_LEN, n_groups)
    gated_a = _sigmoid(za) * ya
    yb = jnp.concatenate(
        [_dot(jnp.concatenate([pooled_p[gi], pooled_s[gi]], axis=0), w_pool[gi])
         for gi in range(n_groups)], axis=-1)
    zb = _dot(xn, w_in[:, o5:])
    h = rows_of(r["xp"], r["xs"]) + _dot(gated_a + _sigmoid(zb) * (yb * pool_scale[...]), w_o[...])

    hn = _rms(h, g_mlp[...]).astype(jnp.bfloat16)
    d_ff = w_up.shape[1]
    acc = h
    mid = _dot(hn, w_up[:, 0:cfg.d_ff_chunk])
    for c0 in range(0, d_ff, cfg.d_ff_chunk):
        nxt = c0 + cfg.d_ff_chunk
        mid_next = _dot(hn, w_up[:, nxt:nxt + cfg.d_ff_chunk]) if nxt < d_ff else None
        act = jnp.square(jnp.maximum(mid, 0.0))
        acc = acc + _dot(act, w_down[c0:nxt, :])
        mid = mid_next
    h = acc

    zg = _dot(_rms(h, g_ple[...]), w_pg[...])
    y = _rms(h + _sigmoid(zg) * pe, g_final[...])
    r["yp"][...] = y[0:L].reshape(1, L, D)
    r["ys"][...] = y[L:].reshape(Ss, Ls, D)

    @pl.when(j == cfg.tiles_per_seq - 1)
    def _():
        r["ncp"][...] = up_ext[:, HU + L - N_CONV_HIST:HU + L, :]
        seq = lax.div(step, cfg.tiles_per_seq)

        def prompt_state_out(i):
            return pltpu.make_async_copy(r["npp_buf"].at[i], r["npp"].at[i, seq],
                                         r["npp_sem"].at[i])

        @pl.when(seq > 0)
        def _():
            for i in range(N_POOL_HIST):
                prompt_state_out(i).wait()

        r["npp_buf"][0:N_POOL_HIST, :] = vp_ext[0, HV + L - N_POOL_HIST:HV + L, :]
        for i in range(N_POOL_HIST):
            prompt_state_out(i).start()

        @pl.when(step == n_steps - 1)
        def _():
            for i in range(N_POOL_HIST):
                prompt_state_out(i).wait()

    r["ncs"][...] = us_ext[:, HU + Ls - N_CONV_HIST:HU + Ls, :]
    @pl.when(step >= 2)
    def _():
        for k in range(n_new):
            pool_state_out(k, slot).wait()

    vs_out[slot, :, 0:n_new, :] = vs_ext[:, HV + Ls - n_new:HV + Ls, :]
    for k in range(n_new):
        pool_state_out(k, slot).start()

    @pl.when(step == n_steps - 1)
    def _():
        for from_slot in ((1 - slot, slot) if n_steps >= 2 else (slot,)):
            for k in range(n_new):
                pool_state_out(k, from_slot).wait()
        if n_kept:
            pool_state_keep().wait()


def _resident(shape):
    zeros = (0,) * len(shape)
    return pl.BlockSpec(shape, lambda *_: zeros, pipeline_mode=pl.Buffered(1))


def _tile_plan(x_prompt, x_sample):
    n_prompt, seq, _ = x_prompt.shape
    n_sample, dec_seq, _ = x_sample.shape
    prompt_rows = 512
    assert seq % prompt_rows == 0 and prompt_rows >= _POOL_HIST
    assert prompt_rows >= 2 * _STAGE_ROWS
    tiles_per_seq = seq // prompt_rows
    n_steps = n_prompt * tiles_per_seq
    assert n_sample % n_steps == 0 and dec_seq % V7X_SUBLANES == 0
    assert (prompt_rows + (n_sample // n_steps) * dec_seq) % (2 * V7X_SUBLANES) == 0
    return _Cfg(n_steps=n_steps, prompt_rows=prompt_rows, tiles_per_seq=tiles_per_seq,
                sample_seqs=n_sample // n_steps, sample_rows=dec_seq, d_ff_chunk=1024)


def kernel(x_prompt, x_sample, state_conv, state_pool, p_prompt, p_sample, g_mix, w_in, w_conv, w_out_conv, w_pool, pool_scale, w_o, g_mlp, w_up, w_down, g_ple, w_ple_gate, w_ple_proj, g_final):
    depth = w_in.shape[0]
    assert depth == 1, "single-layer trunk only"
    small = (g_mix, jnp.swapaxes(w_conv, 0, 1), pool_scale, g_mlp, g_ple, g_final[None, :])
    big = (w_in[0], w_out_conv[0], w_pool[0], w_o[0], w_up[0], w_down[0], w_ple_gate[0],
           w_ple_proj[0])
    cfg = _tile_plan(x_prompt, x_sample)
    L, J, Ss, Ls = cfg.prompt_rows, cfg.tiles_per_seq, cfg.sample_seqs, cfg.sample_rows
    n_prompt, _, D = x_prompt.shape
    n_sample = x_sample.shape[0]
    Dp = p_prompt.shape[-1]
    C = w_conv.shape[-1]
    P = w_pool.shape[1] * w_pool.shape[2]
    any_spec = pl.BlockSpec(memory_space=pl.ANY)
    f32 = x_prompt.dtype

    def prompt_tile(i):
        return (i // J, i % J, 0)

    def prompt_seq(i):
        return (i // J, 0, 0)

    def sample_blk(i):
        return (i, 0, 0)

    state_pool_rows = jnp.swapaxes(state_pool[0], 0, 1)
    assert state_pool_rows.shape == (N_POOL_HIST, n_sample, P)

    in_specs = [pl.BlockSpec((1, L, D), prompt_tile), pl.BlockSpec((1, L, Dp), prompt_tile),
                pl.BlockSpec((Ss, Ls, D), sample_blk), pl.BlockSpec((Ss, Ls, Dp), sample_blk),
                pl.BlockSpec((Ss, N_CONV_HIST, C), sample_blk), any_spec]
    in_specs += [_resident(w.shape) for w in small] + [any_spec for _ in big]

    out_shape = [jax.ShapeDtypeStruct(x_prompt.shape, f32),
                 jax.ShapeDtypeStruct(x_sample.shape, f32),
                 jax.ShapeDtypeStruct((n_prompt, N_CONV_HIST, C), f32),
                 jax.ShapeDtypeStruct((N_POOL_HIST, n_prompt, P), f32),
                 jax.ShapeDtypeStruct((n_sample, N_CONV_HIST, C), f32),
                 jax.ShapeDtypeStruct((N_POOL_HIST, n_sample, P), f32)]
    out_specs = [pl.BlockSpec((1, L, D), prompt_tile), pl.BlockSpec((Ss, Ls, D), sample_blk),
                 pl.BlockSpec((1, N_CONV_HIST, C), prompt_seq),
                 any_spec,
                 pl.BlockSpec((Ss, N_CONV_HIST, C), sample_blk), any_spec]
    scratch = [pltpu.VMEM((1, _CONV_HIST + L, C), jnp.float32),
               pltpu.VMEM((1, _POOL_HIST + L, P), jnp.float32),
               pltpu.VMEM((Ss, _CONV_HIST + Ls, C), jnp.float32),
               pltpu.VMEM((Ss, _POOL_HIST + Ls, P), jnp.float32),
               pltpu.VMEM((2, Ss, _POOL_HIST, P), jnp.float32),
               pltpu.VMEM((2, Ss, _POOL_HIST, P), jnp.float32),
               pltpu.VMEM((_POOL_HIST, P), jnp.float32)]
    scratch += [pltpu.VMEM(w.shape, jnp.bfloat16) for w in big]
    scratch += [pltpu.SemaphoreType.DMA((4,)),
                pltpu.SemaphoreType.DMA((2, N_POOL_HIST)),
                pltpu.SemaphoreType.DMA((2, N_POOL_HIST)),
                pltpu.SemaphoreType.DMA((1,)),
                pltpu.SemaphoreType.DMA((N_POOL_HIST,))]

    y_prompt, y_sample, nc_prompt, np_prompt_rows, nc_sample, np_sample_rows = pl.pallas_call(
        functools.partial(_trunk_kernel, cfg=cfg),
        grid=(cfg.n_steps,),
        in_specs=in_specs,
        out_specs=out_specs,
        out_shape=out_shape,
        scratch_shapes=scratch,
        compiler_params=pltpu.CompilerParams(
            dimension_semantics=("arbitrary",),
            vmem_limit_bytes=_VMEM_LIMIT),
        name="trunk",
    )(x_prompt, p_prompt[0], x_sample, p_sample[0], state_conv[0], state_pool_rows, *small, *big)

    return (y_prompt, y_sample, nc_prompt[None], jnp.swapaxes(np_prompt_rows, 0, 1)[None],
            nc_sample[None], jnp.swapaxes(np_sample_rows, 0, 1)[None])
```
